```python
import jax
import jax.numpy as jnp
from jax import lax
import numpy as np

D_MODEL = 4096
BATCH = 1
SEQ = 8192
DEPTH = 2

NORM_EPS = 1e-6
N_MOD = 6
RET_HEADS = 8
RET_DK = 128
RET_DV = 128
RET_CHUNK = 128
ROPE_BASE = 10000.0
HG_HEADS = 8
HG_DK = 128
HG_DV = 128
HG_CHUNK = 64
RW_HEADS = 16
RW_HD = 64
RW_W_RANK = 64
RW_A_RANK = 64
RW_G_RANK = 128
RW_V_RANK = 32
RW_LN_EPS = 64e-5
FOX_HEADS = 8
FOX_HD = 128
FOX_BLOCK = 128
RET_QK = RET_HEADS * RET_DK
RET_W = RET_HEADS * RET_DV
HG_K = HG_HEADS * HG_DK
HG_W = HG_HEADS * HG_DV
RW_W = RW_HEADS * RW_HD
FOX_W = FOX_HEADS * FOX_HD
N_BRANCH = 4
BRANCH_W = 1024
RW_WIDTHS = (RW_W, RW_W, RW_W, RW_W_RANK, RW_A_RANK, RW_G_RANK)
RW_COLS = sum(RW_WIDTHS)
IN_WIDTHS = (RET_QK, RET_QK, RET_W, RET_W,
             HG_K, HG_K, HG_W, HG_W,
             RW_COLS,
             FOX_W, FOX_W, FOX_W, FOX_HEADS,
             N_BRANCH * D_MODEL)
IN_COLS = sum(IN_WIDTHS)
PEER_HEADS = 8
PEER_NKEYS = 128
PEER_EXPERTS = PEER_NKEYS * PEER_NKEYS
PEER_DKEY = 256
PEER_TOPK = 16
PEER_TOKEN_BLOCK = 64

kernel_name = "hybrid_gated_retention_hgrn2_rwkv7_fox_peer"


def _split_points(widths):
    return np.cumsum(widths)[:-1].tolist()


def rms_norm(x, eps=NORM_EPS):
    xf = x.astype(jnp.float32)
    return (xf * lax.rsqrt(jnp.mean(xf * xf, axis=-1, keepdims=True) + eps)).astype(x.dtype)


def split_heads(x, n):
    b, t, _ = x.shape
    return x.reshape(b, t, n, -1).transpose(0, 2, 1, 3)


def merge_heads(x):
    b, h, t, d = x.shape
    return x.transpose(0, 2, 1, 3).reshape(b, t, h * d)


def rotary(x):
    d, t = x.shape[-1], x.shape[2]
    half = d // 2
    inv = ROPE_BASE ** (-jnp.arange(half, dtype=jnp.float32) / half)
    ang = jnp.arange(t, dtype=jnp.float32)[:, None] * inv[None, :]
    cos, sin = jnp.cos(ang).astype(x.dtype), jnp.sin(ang).astype(x.dtype)
    x1, x2 = x[..., :half], x[..., half:]
    return jnp.concatenate([x1 * cos - x2 * sin, x1 * sin + x2 * cos], axis=-1)


def retention(q, k, v):
    b, h, t, dk = q.shape
    c = RET_CHUNK
    n = t // c
    q = rotary(q)
    k = rotary(k) * (dk ** -0.5)
    log_g = jnp.log(1.0 - 2.0 ** (-5.0 - jnp.arange(h, dtype=jnp.float32)))
    pos = jnp.arange(c, dtype=jnp.float32)
    rel = pos[:, None] - pos[None, :]
    dmat = jnp.where(rel >= 0, jnp.exp(log_g[:, None, None] * jnp.maximum(rel, 0.0)), 0.0).astype(q.dtype)
    xi = jnp.exp(log_g[:, None] * (pos + 1.0)).astype(q.dtype)
    zeta = jnp.exp(log_g[:, None] * (c - 1.0 - pos)).astype(q.dtype)
    gamma_c = jnp.exp(log_g * c).astype(q.dtype)
    qc = q.reshape(b, h, n, c, dk)
    kc = k.reshape(b, h, n, c, dk)
    vc = v.reshape(b, h, n, c, -1)
    scores = jnp.einsum('bhncd,bhnsd->bhncs', qc, kc) * dmat[None, :, None]
    o_inner = jnp.einsum('bhncs,bhnse->bhnce', scores, vc)
    upd = jnp.einsum('bhnsd,bhnse->nbhde', kc * zeta[None, :, None, :, None], vc)

    def step(state, u):
        return state * gamma_c[None, :, None, None] + u, state

    _, prev = lax.scan(step, jnp.zeros_like(upd[0]), upd)
    o_cross = jnp.einsum('bhncd,nbhde->bhnce', qc * xi[None, :, None, :, None], prev)
    return (o_inner + o_cross).reshape(b, h, t, -1)


def hgrn2(q, log_f, k, v):
    b, h, t, dk = q.shape
    c = HG_CHUNK
    n = t // c

    def chunks(a):
        return jnp.moveaxis(a.reshape(b, h, n, c, a.shape[-1]), 2, 0)

    causal = jnp.tril(jnp.ones((c, c), dtype=bool))[:, :, None]

    def step(state, xs):
        qc, lfc, kc, vc = xs
        cum = jnp.cumsum(lfc, axis=2)
        diff = cum[:, :, :, None, :] - cum[:, :, None, :, :]
        decay = jnp.exp(jnp.where(causal, diff, -jnp.inf)).astype(qc.dtype)
        scores = jnp.einsum('bhtk,bhsk,bhtsk->bhts', qc, kc, decay)
        o = (jnp.einsum('bhts,bhse->bhte', scores, vc)
             + jnp.einsum('bhtk,bhke->bhte', qc * jnp.exp(cum).astype(qc.dtype), state))
        last = cum[:, :, -1:, :]
        new_state = (state * jnp.exp(last[:, :, 0, :, None]).astype(state.dtype)
                     + jnp.einsum('bhsk,bhse->bhke', kc * jnp.exp(last - cum).astype(kc.dtype), vc))
        return new_state.astype(state.dtype), o

    state0 = jnp.zeros((b, h, dk, v.shape[-1]), v.dtype)
    _, out = lax.scan(step, state0, (chunks(q), chunks(log_f), chunks(k), chunks(v)))
    return jnp.moveaxis(out, 0, 2).reshape(b, h, t, -1)


def rwkv7_scan(r, w, k, v, a, bb):
    b, t, h, d = r.shape

    def step(state, xs):
        r_t, w_t, k_t, v_t, a_t, b_t = xs
        sa = jnp.einsum('bhij,bhj->bhi', state, a_t)
        state = (state * w_t[:, :, None, :] + sa[..., None] * b_t[:, :, None, :]
                 + v_t[..., None] * k_t[:, :, None, :])
        return state, jnp.einsum('bhij,bhj->bhi', state, r_t)

    xs = tuple(jnp.moveaxis(z, 1, 0) for z in (r, w, k, v, a, bb))
    _, y = lax.scan(step, jnp.zeros((b, h, d, d), r.dtype), xs)
    return jnp.moveaxis(y, 0, 1)


def rwkv7(cols, mu, w0, w2, a0, a2, g2, k_k, k_a, r_k, ln_w, ln_b, v_first, v_res):
    b, t, _ = cols.shape
    prev = jnp.pad(cols, ((0, 0), (1, 0), (0, 0)))[:, :-1]
    cols = cols + (prev - cols) * mu
    r, k, v, wl, al, gl = jnp.split(cols, _split_points(RW_WIDTHS), axis=-1)
    w = -jax.nn.softplus(-(w0 + jnp.tanh(wl) @ w2)) - 0.5
    decay = jnp.exp(-jnp.exp(w.astype(jnp.float32))).astype(cols.dtype)
    a = jax.nn.sigmoid(a0 + al @ a2)
    if v_res is None:
        v_first = v
    else:
        v0, v1, v2 = v_res
        v = v + (v_first - v) * jax.nn.sigmoid(v0 + (v @ v1) @ v2)
    g = jax.nn.sigmoid(gl) @ g2

    def hd(z):
        return z.reshape(b, t, RW_HEADS, RW_HD)

    kkf = hd(k * k_k).astype(jnp.float32)
    kk = (kkf / jnp.maximum(jnp.linalg.norm(kkf, axis=-1, keepdims=True), 1e-12)).astype(k.dtype)
    k = k * (1.0 + (a - 1.0) * k_a)
    rh, kh, vh = hd(r), hd(k), hd(v)
    y = rwkv7_scan(rh, hd(decay), kh, vh, -kk, kk * hd(a))
    yf = y.astype(jnp.float32)
    mean = jnp.mean(yf, axis=-1, keepdims=True)
    var = jnp.mean(jnp.square(yf - mean), axis=-1, keepdims=True)
    yn = ((yf - mean) * lax.rsqrt(var + RW_LN_EPS)).astype(y.dtype).reshape(b, t, RW_W) * ln_w + ln_b
    bonus = (jnp.sum(rh * kh * r_k, axis=-1, keepdims=True) * vh).reshape(b, t, RW_W)
    return (yn + bonus) * g, v_first


def forgetting_attention(q, k, v, log_f):
    b, h, t, d = q.shape
    nb = t // FOX_BLOCK
    cum = jnp.cumsum(log_f, axis=-1)
    qb = jnp.moveaxis(q.reshape(b, h, nb, FOX_BLOCK, d), 2, 0)
    cb = jnp.moveaxis(cum.reshape(b, h, nb, FOX_BLOCK), 2, 0)
    k_pos = jnp.arange(t)

    def block(args):
        qi, ci, i = args
        q_pos = i * FOX_BLOCK + jnp.arange(FOX_BLOCK)
        logits = (jnp.einsum('bhqd,bhkd->bhqk', qi, k).astype(jnp.float32) * (d ** -0.5)
                  + ci[..., None] - cum[:, :, None, :])
        logits = jnp.where(k_pos[None, :] <= q_pos[:, None], logits, -jnp.inf)
        p = jax.nn.softmax(logits, axis=-1).astype(v.dtype)
        return jnp.einsum('bhqk,bhkd->bhqd', p, v)

    out = lax.map(block, (qb, cb, jnp.arange(nb)))
    return jnp.moveaxis(out, 0, 2).reshape(b, h, t, d)


def token_mixer(h, w_in, w_branch, w_out, lb, rw_params, v_res, v_first, fox_fb):
    b, t, _ = h.shape
    proj = h @ w_in
    (rq, rk, rv, rg, hq, hf, hi, hg, rw, fq, fk, fv, ff, gate_cols) = jnp.split(
        proj, _split_points(IN_WIDTHS), axis=-1)
    ya = retention(split_heads(rq, RET_HEADS), split_heads(rk, RET_HEADS), split_heads(rv, RET_HEADS))
    ya = merge_heads(rms_norm(ya)) * jax.nn.silu(rg)
    fgate = lb + (1.0 - lb) * jax.nn.sigmoid(hf)
    log_f = jnp.log(fgate.astype(jnp.float32))
    yb = hgrn2(split_heads(hq, HG_HEADS), split_heads(log_f, HG_HEADS),
               split_heads(1.0 - fgate, HG_HEADS), split_heads(hi, HG_HEADS))
    yb = merge_heads(rms_norm(yb)) * jax.nn.silu(hg)
    yc, v_first = rwkv7(rw, *rw_params, v_first, v_res)
    log_ff = jax.nn.log_sigmoid((ff + fox_fb).astype(jnp.float32)).transpose(0, 2, 1)
    yd = merge_heads(forgetting_attention(split_heads(fq, FOX_HEADS), split_heads(fk, FOX_HEADS),
                                          split_heads(fv, FOX_HEADS), log_ff))
    branches = jnp.stack([ya, yb, yc, yd], axis=2)
    up = jnp.einsum('btnw,nwd->btnd', branches, w_branch)
    gates = jax.nn.sigmoid(gate_cols.reshape(b, t, N_BRANCH, D_MODEL))
    merged = jnp.sum(gates * up, axis=2)
    return merged @ w_out, v_first


def peer(h, w_q, sub_keys, u, v):
    b, t, d = h.shape
    K = PEER_TOPK
    q = (h @ w_q).reshape(b, t, PEER_HEADS, 2, PEER_DKEY // 2)
    s = jnp.einsum('bthpk,hpnk->bthpn', q, sub_keys).astype(jnp.float32)
    top_s, top_i = lax.top_k(s, K)
    cand = (top_s[..., 0, :, None] + top_s[..., 1, None, :]).reshape(b, t, PEER_HEADS, K * K)
    cand_idx = (top_i[..., 0, :, None] * PEER_NKEYS + top_i[..., 1, None, :]).reshape(b, t, PEER_HEADS, K * K)
    best_s, best_j = lax.top_k(cand, K)
    eid = jnp.take_along_axis(cand_idx, best_j, axis=-1)
    gate = jax.nn.softmax(best_s, axis=-1).astype(h.dtype)
    nb = t // PEER_TOKEN_BLOCK
    hb = jnp.moveaxis(h.reshape(b, nb, PEER_TOKEN_BLOCK, d), 1, 0)
    eb = jnp.moveaxis(eid.reshape(b, nb, PEER_TOKEN_BLOCK, PEER_HEADS, K), 1, 0)
    gb = jnp.moveaxis(gate.reshape(b, nb, PEER_TOKEN_BLOCK, PEER_HEADS, K), 1, 0)

    def block(args):
        hx, e, g = args
        ue = jnp.take(u, e, axis=0)
        act = jax.nn.gelu(jnp.einsum('btd,bthkd->bthk', hx, ue), approximate=False) * g
        ve = jnp.take(v, e, axis=0)
        return jnp.einsum('bthk,bthkd->btd', act, ve)

    out = lax.map(block, (hb, eb, gb))
    return jnp.moveaxis(out, 0, 1).reshape(b, t, d)


def setup_inputs(seed: int = 0) -> dict:
    key = jax.random.key(seed)
    ks = jax.random.split(key, 29)
    nrm = jax.random.normal
    f32 = jnp.float32
    L = DEPTH
    d = D_MODEL
    return {
        "x": nrm(ks[0], (BATCH, SEQ, d), f32),
        "c": nrm(ks[1], (BATCH, d), f32),
        "w_in": nrm(ks[2], (L, d, IN_COLS), f32) * (d ** -0.5),
        "w_branch": nrm(ks[3], (L, N_BRANCH, BRANCH_W, d), f32) * (BRANCH_W ** -0.5),
        "w_out": nrm(ks[4], (L, d, d), f32) * (d ** -0.5),
        "ada_w": nrm(ks[5], (d, N_MOD * d), f32) * (0.5 * d ** -0.5),
        "ada_b": 0.01 * nrm(ks[6], (N_MOD * d,), f32),
        "ada_table": 0.1 * nrm(ks[7], (L, N_MOD, d), f32),
        "hg_lb_logits": 0.5 * nrm(ks[8], (L, HG_K), f32),
        "rw_mu": jax.random.uniform(ks[9], (L, RW_COLS), f32),
        "rw_w0": -1.0 + 0.5 * nrm(ks[10], (L, RW_W), f32),
        "rw_w2": nrm(ks[11], (L, RW_W_RANK, RW_W), f32) * (0.5 * RW_W_RANK ** -0.5),
        "rw_a0": 0.1 * nrm(ks[12], (L, RW_W), f32),
        "rw_a2": nrm(ks[13], (L, RW_A_RANK, RW_W), f32) * (0.5 * RW_A_RANK ** -0.5),
        "rw_g2": nrm(ks[14], (L, RW_G_RANK, RW_W), f32) * (RW_G_RANK ** -0.5),
        "rw_k_k": 0.85 + 0.05 * nrm(ks[15], (L, RW_W), f32),
        "rw_k_a": 1.0 + 0.05 * nrm(ks[16], (L, RW_W), f32),
        "rw_r_k": 0.1 * nrm(ks[17], (L, RW_HEADS, RW_HD), f32),
        "rw_ln_w": 1.0 + 0.05 * nrm(ks[18], (L, RW_W), f32),
        "rw_ln_b": 0.01 * nrm(ks[19], (L, RW_W), f32),
        "rw_v0": 0.1 * nrm(ks[20], (L - 1, RW_W), f32),
        "rw_v1": nrm(ks[21], (L - 1, RW_W, RW_V_RANK), f32) * (RW_W ** -0.5),
        "rw_v2": nrm(ks[22], (L - 1, RW_V_RANK, RW_W), f32) * (0.5 * RW_V_RANK ** -0.5),
        "fox_fb": 2.0 + 0.5 * nrm(ks[23], (L, FOX_HEADS), f32),
        "peer_wq": nrm(ks[24], (L, d, PEER_HEADS * PEER_DKEY), f32) * (d ** -0.5),
        "peer_keys": nrm(ks[25], (L, PEER_HEADS, 2, PEER_NKEYS, PEER_DKEY // 2), f32) * ((PEER_DKEY // 2) ** -0.5),
        "peer_u": nrm(ks[26], (L, PEER_EXPERTS, d), f32) * (d ** -0.5),
        "peer_v": nrm(ks[27], (L, PEER_EXPERTS, d), f32) * (PEER_HEADS ** -0.5),
        "final_norm_w": 1.0 + 0.05 * nrm(ks[28], (d,), f32),
    }


def reference(x, c, w_in, w_branch, w_out, ada_w, ada_b, ada_table, hg_lb_logits,
              rw_mu, rw_w0, rw_w2, rw_a0, rw_a2, rw_g2, rw_k_k, rw_k_a, rw_r_k, rw_ln_w, rw_ln_b,
              rw_v0, rw_v1, rw_v2, fox_fb, peer_wq, peer_keys, peer_u, peer_v, final_norm_w):
    b, t, d = x.shape
    mods = (jax.nn.silu(c) @ ada_w + ada_b).reshape(b, N_MOD, d)
    p = jax.nn.softmax(hg_lb_logits.astype(jnp.float32), axis=0)
    cp = jnp.cumsum(p, axis=0)
    lbs = (cp - cp[0]).astype(x.dtype)
    v_first = None
    for l in range(DEPTH):
        m = (mods + ada_table[l])[:, :, None, :]
        sh1, sc1, g1, sh2, sc2, g2 = [m[:, i] for i in range(N_MOD)]
        h = rms_norm(x) * (1.0 + sc1) + sh1
        rw_params = (rw_mu[l], rw_w0[l], rw_w2[l], rw_a0[l], rw_a2[l], rw_g2[l],
                     rw_k_k[l], rw_k_a[l], rw_r_k[l], rw_ln_w[l], rw_ln_b[l])
        v_res = None if l == 0 else (rw_v0[l - 1], rw_v1[l - 1], rw_v2[l - 1])
        y, v_first = token_mixer(h, w_in[l], w_branch[l], w_out[l], lbs[l], rw_params, v_res,
                                 v_first, fox_fb[l])
        x = x + g1 * y
        h = rms_norm(x) * (1.0 + sc2) + sh2
        x = x + g2 * peer(h, peer_wq[l], peer_keys[l], peer_u[l], peer_v[l])
    return rms_norm(x) * final_norm_w
```

```python
import functools

import jax
import jax.numpy as jnp
import numpy as np
from jax import lax
from jax.experimental import pallas as pl
from jax.experimental.pallas import tpu as pltpu

F32, BF16, I32 = jnp.float32, jnp.bfloat16, jnp.int32

D_MODEL = 4096
NORM_EPS = 1e-6
N_MOD = 6
RET_HEADS, RET_DK, RET_CHUNK, ROPE_BASE = 8, 128, 128, 10000.0
HG_HEADS, HG_DK = 8, 128
HG_SUB = 16
RW_HEADS, RW_HD = 16, 64
RW_W = RW_HEADS * RW_HD
RW_CHUNK = 64
RW_LN_EPS = 64e-5
FOX_HEADS, FOX_HD = 8, 128
N_BRANCH, BRANCH_W = 4, 1024
PEER_HEADS, PEER_NKEYS, PEER_DKEY, PEER_TOPK = 8, 128, 256, 16
PEER_EXPERTS = PEER_NKEYS * PEER_NKEYS
COL_RW = 8192
RW_COLS = 3 * RW_W + 64 + 64 + 128
COL_FOX = COL_RW + RW_COLS
COL_FF = COL_FOX + 3 * FOX_HEADS * FOX_HD
COL_GATE = COL_FF + FOX_HEADS

V7X_VMEM_LIMIT_BYTES = 56 * 1024 * 1024

_NN = (((1,), (0,)), ((), ()))
_NT = (((1,), (1,)), ((), ()))
_TN = (((0,), (0,)), ((), ()))


def _dot(a, b, dims=_NN):
    return lax.dot_general(a.astype(BF16), b.astype(BF16), dims, preferred_element_type=F32)


def _dot_sel(sel, x, x_first=False):
    hi = x.astype(BF16)
    r1 = x - hi.astype(F32)
    mid = r1.astype(BF16)
    lo = (r1 - mid.astype(F32)).astype(BF16)
    s = sel.astype(BF16)
    out = None
    for part in (hi, mid, lo):
        ab = (part, s) if x_first else (s, part)
        term = lax.dot_general(*ab, _NN, preferred_element_type=F32)
        out = term if out is None else out + term
    return out


def _params(*sem):
    return pltpu.CompilerParams(dimension_semantics=sem, vmem_limit_bytes=V7X_VMEM_LIMIT_BYTES)


def _mm_kernel(*refs, pre, has_bias, has_res):
    a_ref, w_ref = refs[0], refs[1]
    k = 2
    a = a_ref[...]
    if pre == "silu":
        a = a * jax.nn.sigmoid(a)
    y = _dot(a, w_ref[...])
    if has_bias:
        y = y + refs[k][...]
        k += 1
    if has_res:
        y = refs[k][...] + refs[k + 1][...] * y
        k += 2
    refs[k][...] = y.astype(refs[k].dtype)


def _matmul(a, w, *, n_out, tm, tn, out_dtype, layer=None, col_block0=0, pre=None, bias=None, res=None, gate=None):
    m, kdim = a.shape
    assert m % tm == 0
    grid = (m // tm, pl.cdiv(n_out, tn))
    if layer is None:
        w_spec = pl.BlockSpec((kdim, tn), lambda i, j: (0, j + col_block0))
    else:
        w_spec = pl.BlockSpec((None, kdim, tn), lambda i, j: (layer, 0, j + col_block0))
    in_specs = [pl.BlockSpec((tm, kdim), lambda i, j: (i, 0)), w_spec]
    args = [a, w]
    if bias is not None:
        in_specs.append(pl.BlockSpec((1, tn), lambda i, j: (0, j)))
        args.append(bias)
    if res is not None:
        in_specs += [pl.BlockSpec((tm, tn), lambda i, j: (i, j)), pl.BlockSpec((1, tn), lambda i, j: (0, j))]
        args += [res, gate]
    return pl.pallas_call(
        functools.partial(_mm_kernel, pre=pre, has_bias=bias is not None, has_res=res is not None),
        grid=grid,
        in_specs=in_specs,
        out_specs=pl.BlockSpec((tm, tn), lambda i, j: (i, j)),
        out_shape=jax.ShapeDtypeStruct((m, n_out), out_dtype),
        compiler_params=_params("parallel", "parallel"),
    )(*args)


def _mm_acc_kernel(a_ref, w_ref, x_ref, g_ref, o_ref, acc_ref):
    @pl.when(pl.program_id(2) == 0)
    def _():
        acc_ref[...] = jnp.zeros_like(acc_ref)

    acc_ref[...] += _dot(a_ref[...], w_ref[...])

    @pl.when(pl.program_id(2) == pl.num_programs(2) - 1)
    def _():
        o_ref[...] = x_ref[...] + g_ref[...] * acc_ref[...]


def _matmul_acc_res(a, w, layer, res, gate, *, tm, tn, tk):
    m, kdim = a.shape
    n = w.shape[2]
    return pl.pallas_call(
        _mm_acc_kernel,
        grid=(m // tm, n // tn, kdim // tk),
        in_specs=[
            pl.BlockSpec((tm, tk), lambda i, j, k: (i, k)),
            pl.BlockSpec((None, tk, tn), lambda i, j, k: (layer, k, j)),
            pl.BlockSpec((tm, tn), lambda i, j, k: (i, j)),
            pl.BlockSpec((1, tn), lambda i, j, k: (0, j)),
        ],
        out_specs=pl.BlockSpec((tm, tn), lambda i, j, k: (i, j)),
        out_shape=jax.ShapeDtypeStruct((m, n), F32),
        scratch_shapes=[pltpu.VMEM((tm, tn), F32)],
        compiler_params=_params("parallel", "parallel", "arbitrary"),
    )(a, w, res, gate)


def _norm_kernel(x_ref, mul_ref, add_ref, o_ref):
    x = x_ref[...]
    ms = jnp.mean(x * x, axis=-1, keepdims=True)
    o_ref[...] = (x * lax.rsqrt(ms + NORM_EPS) * mul_ref[...] + add_ref[...]).astype(o_ref.dtype)


def _rms_mod(x, mul, add, out_dtype, tm=256):
    m, d = x.shape
    return pl.pallas_call(
        _norm_kernel,
        grid=(m // tm,),
        in_specs=[pl.BlockSpec((tm, d), lambda i: (i, 0)), pl.BlockSpec((1, d), lambda i: (0, 0)),
                  pl.BlockSpec((1, d), lambda i: (0, 0))],
        out_specs=pl.BlockSpec((tm, d), lambda i: (i, 0)),
        out_shape=jax.ShapeDtypeStruct((m, d), out_dtype),
        compiler_params=_params("parallel"),
    )(x, mul, add)


def _hg_lb_kernel(lg_ref, o_ref):
    lg = lg_ref[...]
    e = jnp.exp(lg - jnp.max(lg, axis=0, keepdims=True))
    p = e / jnp.sum(e, axis=0, keepdims=True)
    rows = [jnp.zeros_like(p[0:1])]
    for l in range(1, lg.shape[0]):
        rows.append(rows[-1] + p[l:l + 1])
    o_ref[...] = jnp.concatenate(rows, axis=0)


def _hg_lower_bounds(logits):
    return pl.pallas_call(_hg_lb_kernel, out_shape=jax.ShapeDtypeStruct(logits.shape, F32))(logits)


def _ret_kernel(q_ref, k_ref, v_ref, g_ref, cos_ref, sin_ref, dmat_ref, xi_ref, zeta_ref, gam_ref, o_ref, st_ref, *, tb):
    c = RET_CHUNK

    @pl.when(pl.program_id(1) == 0)
    def _():
        st_ref[...] = jnp.zeros_like(st_ref)

    dmat, xi, zeta, gam = dmat_ref[...], xi_ref[...], zeta_ref[...], gam_ref[...]
    for ci in range(tb // c):
        sl = slice(ci * c, (ci + 1) * c)
        q, k, v = q_ref[sl, :], k_ref[sl, :], v_ref[sl, :]
        cos, sin = cos_ref[sl, :], sin_ref[sl, :]
        half = RET_DK // 2
        qr = q * cos + pltpu.roll(q, half, 1) * sin
        kr = (k * cos + pltpu.roll(k, half, 1) * sin) * (RET_DK ** -0.5)
        scores = _dot(qr, kr, _NT) * dmat
        o = _dot(scores, v)
        st = st_ref[...]
        o = o + _dot(qr * xi, st)
        st_ref[...] = st * gam + _dot(kr * zeta, v, _TN)
        ms = jnp.mean(o * o, axis=-1, keepdims=True)
        g = g_ref[sl, :]
        y = o * lax.rsqrt(ms + NORM_EPS) * (g * jax.nn.sigmoid(g))
        o_ref[sl, :] = y.astype(o_ref.dtype)


def _retention(proj, tables, tb=512):
    t = proj.shape[0]
    h = RET_HEADS
    cos, sin, dmat, xi, zeta, gam = tables
    c = RET_CHUNK
    col = lambda g: pl.BlockSpec((tb, 128), lambda hh, n, g=g: (n, g * h + hh))
    tab = pl.BlockSpec((tb, 128), lambda hh, n: (n, 0))
    return pl.pallas_call(
        functools.partial(_ret_kernel, tb=tb),
        grid=(h, t // tb),
        in_specs=[col(0), col(1), col(2), col(3), tab, tab,
                  pl.BlockSpec((None, c, c), lambda hh, n: (hh, 0, 0)),
                  pl.BlockSpec((None, c, 128), lambda hh, n: (hh, 0, 0)),
                  pl.BlockSpec((None, c, 128), lambda hh, n: (hh, 0, 0)),
                  pl.BlockSpec((None, 1, 128), lambda hh, n: (hh, 0, 0))],
        out_specs=pl.BlockSpec((tb, 128), lambda hh, n: (n, hh)),
        out_shape=jax.ShapeDtypeStruct((t, h * 128), BF16),
        scratch_shapes=[pltpu.VMEM((RET_DK, 128), F32)],
        compiler_params=_params("parallel", "arbitrary"),
    )(proj, proj, proj, proj, cos, sin, dmat, xi, zeta, gam)


def _retention_tables(t):
    half = RET_DK // 2
    inv = ROPE_BASE ** (-jnp.arange(half, dtype=F32) / half)
    ang = jnp.arange(t, dtype=F32)[:, None] * inv[None, :]
    cos, sin = jnp.cos(ang), jnp.sin(ang)
    cos2 = jnp.concatenate([cos, cos], axis=-1)
    sin2 = jnp.concatenate([-sin, sin], axis=-1)
    c = RET_CHUNK
    log_g = jnp.log(1.0 - 2.0 ** (-5.0 - jnp.arange(RET_HEADS, dtype=F32)))
    pos = jnp.arange(c, dtype=F32)
    rel = pos[:, None] - pos[None, :]
    dmat = jnp.where(rel >= 0, jnp.exp(log_g[:, None, None] * jnp.maximum(rel, 0.0)), 0.0)
    xi = jnp.broadcast_to(jnp.exp(log_g[:, None] * (pos + 1.0))[:, :, None], (RET_HEADS, c, 128))
    zeta = jnp.broadcast_to(jnp.exp(log_g[:, None] * (c - 1.0 - pos))[:, :, None], (RET_HEADS, c, 128))
    gam = jnp.broadcast_to(jnp.exp(log_g * c)[:, None, None], (RET_HEADS, 1, 128))
    return cos2, sin2, dmat, xi, zeta, gam


def _hg_kernel(q_ref, f_ref, v_ref, g_ref, lb_ref, o_ref, st_ref, *, tb):
    n = HG_SUB

    @pl.when(pl.program_id(1) == 0)
    def _():
        st_ref[...] = jnp.zeros_like(st_ref)

    lb = lb_ref[...]
    row = lax.broadcasted_iota(I32, (n, 128), 0)
    tri = (lax.broadcasted_iota(I32, (n, n), 0) >= lax.broadcasted_iota(I32, (n, n), 1)).astype(F32)

    def body(i, carry):
        r0 = pl.multiple_of(i * n, n)
        sl = pl.ds(r0, n)
        q, v = q_ref[sl, :], v_ref[sl, :]
        fg = lb + (1.0 - lb) * jax.nn.sigmoid(f_ref[sl, :])
        lf = jnp.log(fg)
        kk = 1.0 - fg
        cum = _dot_sel(tri, lf)
        o = jnp.zeros((n, 128), F32)
        for s in range(n):
            d = jnp.where(row >= s, cum - cum[s:s + 1, :], -jnp.inf)
            col = jnp.sum(q * jnp.exp(d) * kk[s:s + 1, :], axis=-1, keepdims=True)
            o = o + col * v[s:s + 1, :]
        st = st_ref[...]
        o = o + _dot(q * jnp.exp(cum), st, _NT)
        last = cum[n - 1:n, :]
        st_ref[...] = st * jnp.exp(last) + _dot(v, kk * jnp.exp(last - cum), _TN)
        ms = jnp.mean(o * o, axis=-1, keepdims=True)
        g = g_ref[sl, :]
        o_ref[sl, :] = (o * lax.rsqrt(ms + NORM_EPS) * (g * jax.nn.sigmoid(g))).astype(o_ref.dtype)
        return carry

    lax.fori_loop(0, tb // n, body, 0)


def _hgrn2(proj, lb, tb=512):
    t = proj.shape[0]
    h = HG_HEADS
    col = lambda g: pl.BlockSpec((tb, 128), lambda hh, n, g=g: (n, (4 + g) * h + hh))
    return pl.pallas_call(
        functools.partial(_hg_kernel, tb=tb),
        grid=(h, t // tb),
        in_specs=[col(0), col(1), col(2), col(3), pl.BlockSpec((1, 128), lambda hh, n: (0, hh))],
        out_specs=pl.BlockSpec((tb, 128), lambda hh, n: (n, hh)),
        out_shape=jax.ShapeDtypeStruct((t, h * 128), BF16),
        scratch_shapes=[pltpu.VMEM((128, HG_DK), F32)],
        compiler_params=_params("parallel", "arbitrary"),
    )(proj, proj, proj, proj, lb)


def _rw_prep_kernel(*refs, has_vres):
    if has_vres:
        (cur_ref, prv_ref, mu_ref, w0_ref, w2_ref, a0_ref, a2_ref, g2_ref, vf_ref, v0_ref, v1_ref, v2_ref,
         r_ref, ld_ref, k_ref, v_ref, a_ref, g_ref) = refs
    else:
        (cur_ref, prv_ref, mu_ref, w0_ref, w2_ref, a0_ref, a2_ref, g2_ref,
         r_ref, ld_ref, k_ref, v_ref, a_ref, g_ref) = refs
    cur = cur_ref[...]
    tt = cur.shape[0]
    prev_row = jnp.where(pl.program_id(0) == 0, 0.0, prv_ref[7:8, :])
    row = lax.broadcasted_iota(I32, cur.shape, 0)
    shifted = jnp.where(row == 0, prev_row, pltpu.roll(cur, 1, 0))
    cols = cur + (shifted - cur) * mu_ref[...]
    w = RW_W
    r, k, v = cols[:, 0:w], cols[:, w:2 * w], cols[:, 2 * w:3 * w]
    wa = cols[:, 3 * w:3 * w + 128]
    lane = lax.broadcasted_iota(I32, wa.shape, 1)
    wa = jnp.where(lane < 64, jnp.tanh(wa), wa)
    gl = cols[:, 3 * w + 128:3 * w + 256]
    wlog = -jax.nn.softplus(-(w0_ref[...] + _dot(wa, w2_ref[...]))) - 0.5
    a = jax.nn.sigmoid(a0_ref[...] + _dot(wa, a2_ref[...]))
    if has_vres:
        mix = jax.nn.sigmoid(v0_ref[...] + _dot(_dot(v, v1_ref[...]), v2_ref[...]))
        v = v + (vf_ref[...] - v) * mix
    r_ref[...] = r
    ld_ref[...] = -jnp.exp(wlog)
    k_ref[...] = k
    v_ref[...] = v
    a_ref[...] = a
    g_ref[...] = _dot(jax.nn.sigmoid(gl), g2_ref[...])


def _rw_prep(cols, mu, w0, w2p, a0, a2p, g2, vres, tt=256):
    t, nc = cols.shape
    w = RW_W
    full = lambda shape: pl.BlockSpec(shape, lambda i: (0,) * len(shape))
    in_specs = [pl.BlockSpec((tt, nc), lambda i: (i, 0)),
                pl.BlockSpec((8, nc), lambda i: (jnp.maximum(i * (tt // 8) - 1, 0), 0)),
                full((1, nc)), full((1, w)), full((128, w)), full((1, w)), full((128, w)), full((128, w))]
    args = [cols, cols, mu, w0, w2p, a0, a2p, g2]
    if vres is not None:
        v_first, v0, v1p, v2p = vres
        in_specs += [pl.BlockSpec((tt, w), lambda i: (i, 0)), full((1, w)), full((w, 128)), full((128, w))]
        args += [v_first, v0, v1p, v2p]
    out = jax.ShapeDtypeStruct((t, w), F32)
    return pl.pallas_call(
        functools.partial(_rw_prep_kernel, has_vres=vres is not None),
        grid=(t // tt,),
        in_specs=in_specs,
        out_specs=[pl.BlockSpec((tt, w), lambda i: (i, 0))] * 6,
        out_shape=[out] * 6,
        compiler_params=_params("parallel"),
    )(*args)


def _rw_level_masks(c):
    ti = lax.broadcasted_iota(I32, (c, c), 0)
    si = lax.broadcasted_iota(I32, (c, c), 1)
    masks = []
    n = 1
    while n < c:
        same = (ti // (2 * n)) == (si // (2 * n))
        masks.append(same & ((ti // n) % 2 == 1) & ((si // n) % 2 == 0))
        n *= 2
    return ti, si, masks


def _rw_head_chunk(r, ld, k, v, a, kk_w, ka_w, masks_pack):
    ti, si, masks = masks_pack
    c = r.shape[0]
    tril = (ti >= si)
    strict = (ti > si)
    cw = _dot_sel(tril.astype(F32), ld)
    wc = jnp.exp(cw)
    wprev = jnp.exp(cw - ld)
    winv = jnp.exp(-cw)
    wc_last = wc[c - 1:c, :]
    wtail = wc_last * winv
    kkf = k * kk_w
    nrm = jnp.sqrt(jnp.sum(kkf * kkf, axis=-1, keepdims=True))
    kk = kkf / jnp.maximum(nrm, 1e-12)
    k2 = k * (1.0 + (a - 1.0) * ka_w)
    bv = kk * a
    at = -kk * wprev
    rt = r * wc
    gram = _dot(jnp.concatenate([at, rt], axis=0), jnp.concatenate([bv * winv, k2 * winv], axis=0), _NT)
    lab = jnp.where(strict, gram[:c, :c], 0.0)
    lak = jnp.where(strict, gram[:c, c:], 0.0)
    mrb = jnp.where(tril, gram[c:, :c], 0.0)
    mrk = jnp.where(tril, gram[c:, c:], 0.0)
    x = jnp.where(ti == si, 1.0, 0.0) + jnp.where(masks[0], lab, 0.0)
    for m in masks[1:]:
        off = jnp.where(m, lab, 0.0)
        x = x + _dot(_dot(x, off), x)
    lv = _dot(jnp.concatenate([lak, mrk], axis=0), v)
    pq = _dot(x, jnp.concatenate([at, lv[:c]], axis=1))
    mpq = _dot(mrb, pq)
    r2 = rt + mpq[:, :RW_HD]
    y0 = mpq[:, RW_HD:] + lv[c:]
    bh = bv * wtail
    kh = k2 * wtail
    g = _dot(pq[:, :RW_HD], bh, _TN)
    s_add = _dot(jnp.concatenate([pq[:, RW_HD:], v], axis=0), jnp.concatenate([bh, kh], axis=0), _TN)
    return r2, y0, g, s_add, wc_last, k2


def _rw_kernel(r_ref, ld_ref, k_ref, v_ref, a_ref, g_ref, kk_ref, ka_ref, rk_ref, lnw_ref, lnb_ref, o_ref, st_ref, *, tb):
    c = RW_CHUNK

    @pl.when(pl.program_id(1) == 0)
    def _():
        st_ref[...] = jnp.zeros_like(st_ref)

    masks_pack = _rw_level_masks(c)
    for hh in range(2):
        hs = slice(hh * RW_HD, (hh + 1) * RW_HD)
        kk_w, ka_w, rk_w = kk_ref[:, hs], ka_ref[:, hs], rk_ref[:, hs]
        lnw, lnb = lnw_ref[:, hs], lnb_ref[:, hs]
        pre = []
        for ci in range(tb // c):
            sl = slice(ci * c, (ci + 1) * c)
            r, ld, k, v, a = r_ref[sl, hs], ld_ref[sl, hs], k_ref[sl, hs], v_ref[sl, hs], a_ref[sl, hs]
            pre.append((r, v) + _rw_head_chunk(r, ld, k, v, a, kk_w, ka_w, masks_pack))
        s = st_ref[hh]
        for ci, (r, v, r2, y0, g, s_add, wc_last, k2) in enumerate(pre):
            sl = slice(ci * c, (ci + 1) * c)
            y = _dot(r2, s, _NT) + y0
            s = s * wc_last + _dot(s, g) + s_add
            mean = jnp.mean(y, axis=-1, keepdims=True)
            var = jnp.mean(jnp.square(y - mean), axis=-1, keepdims=True)
            yn = (y - mean) * lax.rsqrt(var + RW_LN_EPS) * lnw + lnb
            bonus = jnp.sum(r * k2 * rk_w, axis=-1, keepdims=True) * v
            o_ref[sl, hs] = ((yn + bonus) * g_ref[sl, hs]).astype(o_ref.dtype)
        st_ref[hh] = s


def _rwkv7(prep, k_k, k_a, r_k, ln_w, ln_b, tb=256):
    r, ld, k, v, a, g = prep
    t = r.shape[0]
    npair = RW_W // 128
    col = pl.BlockSpec((tb, 128), lambda p, n: (n, p))
    par = pl.BlockSpec((1, 128), lambda p, n: (0, p))
    return pl.pallas_call(
        functools.partial(_rw_kernel, tb=tb),
        grid=(npair, t // tb),
        in_specs=[col] * 6 + [par] * 5,
        out_specs=col,
        out_shape=jax.ShapeDtypeStruct((t, RW_W), BF16),
        scratch_shapes=[pltpu.VMEM((2, RW_HD, RW_HD), F32)],
        compiler_params=_params("parallel", "arbitrary"),
    )(r, ld, k, v, a, g, k_k, k_a, r_k, ln_w, ln_b)


def _fox_cum_kernel(h_ref, wt_ref, w_ref, fbc_ref, fbr_ref, row_ref, col_ref, crow_ref, ccol_ref):
    @pl.when(pl.program_id(0) == 0)
    def _():
        crow_ref[...] = jnp.zeros_like(crow_ref)
        ccol_ref[...] = jnp.zeros_like(ccol_ref)

    h = h_ref[...]
    tt = h.shape[0]
    ti = lax.broadcasted_iota(I32, (tt, tt), 0)
    si = lax.broadcasted_iota(I32, (tt, tt), 1)
    lf_r = jax.nn.log_sigmoid(_dot(wt_ref[...], h, _NT) + fbc_ref[...])
    cum_r = _dot_sel((ti <= si).astype(F32), lf_r, x_first=True) + crow_ref[:, 0:1]
    row_ref[...] = cum_r
    crow_ref[...] = jnp.broadcast_to(cum_r[:, tt - 1:tt], crow_ref.shape)
    lf_c = jax.nn.log_sigmoid(_dot(h, w_ref[...]) + fbr_ref[...])
    cum_c = _dot_sel((ti >= si).astype(F32), lf_c) + ccol_ref[0:1, :]
    col_ref[...] = cum_c
    ccol_ref[...] = jnp.broadcast_to(cum_c[tt - 1:tt, :], ccol_ref.shape)


def _fox_cum(h, w_ff, fb, tt=512):
    t, d = h.shape
    nh = FOX_HEADS
    full = lambda shape: pl.BlockSpec(shape, lambda i: (0,) * len(shape))
    return pl.pallas_call(
        _fox_cum_kernel,
        grid=(t // tt,),
        in_specs=[pl.BlockSpec((tt, d), lambda i: (i, 0)), full((nh, d)), full((d, nh)), full((nh, 1)), full((1, nh))],
        out_specs=[pl.BlockSpec((nh, tt), lambda i: (0, i)), pl.BlockSpec((tt, nh), lambda i: (i, 0))],
        out_shape=[jax.ShapeDtypeStruct((nh, t), F32), jax.ShapeDtypeStruct((t, nh), F32)],
        scratch_shapes=[pltpu.VMEM((nh, 128), F32), pltpu.VMEM((8, nh), F32)],
        compiler_params=_params("arbitrary"),
    )(h, w_ff.T, w_ff, fb.reshape(nh, 1), fb.reshape(1, nh))


def _fox_kernel(q_ref, k_ref, v_ref, cq_ref, ck_ref, o_ref, m_ref, l_ref, acc_ref, cqc_ref, *, tq):
    hh, qi, ki = pl.program_id(0), pl.program_id(1), pl.program_id(2)

    @pl.when(ki == 0)
    def _():
        m_ref[...] = jnp.full_like(m_ref, -jnp.inf)
        l_ref[...] = jnp.zeros_like(l_ref)
        acc_ref[...] = jnp.zeros_like(acc_ref)
        lane = lax.broadcasted_iota(I32, cq_ref.shape, 1)
        cqc_ref[...] = jnp.sum(jnp.where(lane == hh, cq_ref[...], 0.0), axis=-1, keepdims=True)

    @pl.when(ki <= qi)
    def _():
        s = _dot(q_ref[...], k_ref[...], _NT) * (FOX_HD ** -0.5)
        s = s + cqc_ref[...] - ck_ref[pl.ds(hh, 1), :]
        ri = lax.broadcasted_iota(I32, s.shape, 0)
        cj = lax.broadcasted_iota(I32, s.shape, 1)
        s = jnp.where((ki < qi) | (cj <= ri), s, -jnp.inf)
        m_old = m_ref[...]
        m_new = jnp.maximum(m_old, jnp.max(s, axis=-1, keepdims=True))
        alpha = jnp.exp(m_old - m_new)
        p = jnp.exp(s - m_new)
        l_ref[...] = alpha * l_ref[...] + jnp.sum(p, axis=-1, keepdims=True)
        acc_ref[...] = alpha * acc_ref[...] + _dot(p, v_ref[...])
        m_ref[...] = m_new

    @pl.when(ki == qi)
    def _():
        o_ref[...] = (acc_ref[...] / l_ref[...]).astype(o_ref.dtype)


def _fox_attention(qkv, cum_row, cum_col, tq=512):
    t = qkv.shape[0]
    h = FOX_HEADS
    nq = t // tq
    return pl.pallas_call(
        functools.partial(_fox_kernel, tq=tq),
        grid=(h, nq, nq),
        in_specs=[pl.BlockSpec((tq, 128), lambda hh, qi, ki: (qi, hh)),
                  pl.BlockSpec((tq, 128), lambda hh, qi, ki: (jnp.minimum(ki, qi), h + hh)),
                  pl.BlockSpec((tq, 128), lambda hh, qi, ki: (jnp.minimum(ki, qi), 2 * h + hh)),
                  pl.BlockSpec((tq, h), lambda hh, qi, ki: (qi, 0)),
                  pl.BlockSpec((h, tq), lambda hh, qi, ki: (0, jnp.minimum(ki, qi)))],
        out_specs=pl.BlockSpec((tq, 128), lambda hh, qi, ki: (qi, hh)),
        out_shape=jax.ShapeDtypeStruct((t, h * 128), BF16),
        scratch_shapes=[pltpu.VMEM((tq, 1), F32), pltpu.VMEM((tq, 1), F32), pltpu.VMEM((tq, 128), F32),
                        pltpu.VMEM((tq, 1), F32)],
        compiler_params=_params("parallel", "parallel", "arbitrary"),
    )(qkv, qkv, qkv, cum_col, cum_row)


def _merge_kernel(ya_ref, yb_ref, yc_ref, yd_ref, wa_ref, wb_ref, wc_ref, wd_ref, ga_ref, gb_ref, gc_ref, gd_ref, o_ref):
    acc = jax.nn.sigmoid(ga_ref[...]) * _dot(ya_ref[...], wa_ref[...])
    acc += jax.nn.sigmoid(gb_ref[...]) * _dot(yb_ref[...], wb_ref[...])
    acc += jax.nn.sigmoid(gc_ref[...]) * _dot(yc_ref[...], wc_ref[...])
    acc += jax.nn.sigmoid(gd_ref[...]) * _dot(yd_ref[...], wd_ref[...])
    o_ref[...] = acc.astype(o_ref.dtype)


def _merge(branches, w_branch, layer, gates, tm=1024, tn=512):
    t = branches[0].shape[0]
    d = D_MODEL
    nb = d // tn
    y_spec = pl.BlockSpec((tm, BRANCH_W), lambda i, j: (i, 0))
    w_specs = [pl.BlockSpec((None, None, BRANCH_W, tn), lambda i, j, n=n: (layer, n, 0, j)) for n in range(N_BRANCH)]
    g_specs = [pl.BlockSpec((tm, tn), lambda i, j, n=n: (i, n * nb + j)) for n in range(N_BRANCH)]
    return pl.pallas_call(
        _merge_kernel,
        grid=(t // tm, nb),
        in_specs=[y_spec] * 4 + w_specs + g_specs,
        out_specs=pl.BlockSpec((tm, tn), lambda i, j: (i, j)),
        out_shape=jax.ShapeDtypeStruct((t, d), BF16),
        compiler_params=_params("parallel", "parallel"),
    )(*branches, w_branch, w_branch, w_branch, w_branch, gates, gates, gates, gates)


def _topk_rows(s, k):
    n = s.shape[0]
    idx_iota = lax.broadcasted_iota(I32, s.shape, 0)
    vals, idxs = [], []
    for _ in range(k):
        m = jnp.max(s, axis=0, keepdims=True)
        idx = jnp.min(jnp.where(s == m, idx_iota, n), axis=0, keepdims=True)
        vals.append(m)
        idxs.append(idx)
        s = jnp.where(idx_iota == idx, -jnp.inf, s)
    return jnp.concatenate(vals, axis=0), jnp.concatenate(idxs, axis=0)


def _peer_topk_kernel(q_ref, keys_ref, a_ref, b_ref, g_ref):
    kq = PEER_TOPK
    half = PEER_DKEY // 2
    a_rows, b_rows, g_rows = [], [], []
    for hh in range(PEER_HEADS):
        top = []
        for p in range(2):
            qs = q_ref[:, (2 * hh + p) * half:(2 * hh + p + 1) * half]
            s = _dot(keys_ref[hh, p], qs, _NT)
            top.append(_topk_rows(s, kq))
        (v1, i1), (v2, i2) = top
        cand = jnp.concatenate([v1[r:r + 1, :] + v2 for r in range(kq)], axis=0)
        best, j = _topk_rows(cand, kq)
        r1, r2 = j // kq, j % kq
        e1 = jnp.zeros_like(j)
        e2 = jnp.zeros_like(j)
        for r in range(kq):
            e1 = e1 + jnp.where(r1 == r, i1[r:r + 1, :], 0)
            e2 = e2 + jnp.where(r2 == r, i2[r:r + 1, :], 0)
        ex = jnp.exp(best - best[0:1, :])
        a_rows.append(e1)
        b_rows.append(e2)
        g_rows.append(ex / jnp.sum(ex, axis=0, keepdims=True))
    a_ref[...] = jnp.concatenate(a_rows, axis=0).T
    b_ref[...] = jnp.concatenate(b_rows, axis=0).T
    g_ref[...] = jnp.concatenate(g_rows, axis=0).T


def _peer_topk(q, keys, tt=256):
    t = q.shape[0]
    n = PEER_HEADS * PEER_TOPK
    spec = pl.BlockSpec((tt, n), lambda i: (i, 0))
    return pl.pallas_call(
        _peer_topk_kernel,
        grid=(t // tt,),
        in_specs=[pl.BlockSpec((tt, q.shape[1]), lambda i: (i, 0)),
                  pl.BlockSpec(keys.shape, lambda i: (0, 0, 0, 0))],
        out_specs=[spec, spec, spec],
        out_shape=[jax.ShapeDtypeStruct((t, n), I32), jax.ShapeDtypeStruct((t, n), I32), jax.ShapeDtypeStruct((t, n), F32)],
        compiler_params=_params("parallel"),
    )(q, keys)


def _peer_weights_kernel(a_ref, b_ref, g_ref, o_ref):
    tt = a_ref.shape[0]
    nk = PEER_NKEYS
    sub = lax.broadcasted_iota(I32, (nk, a_ref.shape[1]), 0)

    def body(t, carry):
        a = a_ref[pl.ds(t, 1), :]
        b = b_ref[pl.ds(t, 1), :]
        g = g_ref[pl.ds(t, 1), :]
        g_hi = g.astype(BF16).astype(F32)
        g_lo = g - g_hi
        left = jnp.concatenate([jnp.where(sub == a, g_hi, 0.0), jnp.where(sub == a, g_lo, 0.0)], axis=1)
        onehot = jnp.where(sub == b, 1.0, 0.0)
        right = jnp.concatenate([onehot, onehot], axis=1)
        o_ref[t] = _dot(left, right, _NT)
        return carry

    lax.fori_loop(0, tt, body, 0)


def _peer_weights(a, b, g, tt=64):
    t, n = a.shape
    spec = pl.BlockSpec((tt, n), lambda i: (i, 0))
    return pl.pallas_call(
        _peer_weights_kernel,
        grid=(t // tt,),
        in_specs=[spec, spec, spec],
        out_specs=pl.BlockSpec((tt, PEER_NKEYS, PEER_NKEYS), lambda i: (i, 0, 0)),
        out_shape=jax.ShapeDtypeStruct((t, PEER_NKEYS, PEER_NKEYS), F32),
        compiler_params=_params("parallel"),
    )(a, b, g)


def _peer_act_kernel(h_ref, u_ref, w_ref, o_ref, *, ne1):
    z = _dot(h_ref[...], u_ref[...], _NT)
    w = jnp.concatenate([w_ref[:, j, :] for j in range(ne1)], axis=1)
    gelu = 0.5 * z * (1.0 + lax.erf(z * (2.0 ** -0.5)))
    o_ref[...] = (gelu * w).astype(o_ref.dtype)


def _peer_act(h, u, layer, wts, tm=512, te=1024):
    t, d = h.shape
    e = u.shape[1]
    ne1 = te // PEER_NKEYS
    return pl.pallas_call(
        functools.partial(_peer_act_kernel, ne1=ne1),
        grid=(t // tm, e // te),
        in_specs=[pl.BlockSpec((tm, d), lambda i, j: (i, 0)),
                  pl.BlockSpec((None, te, d), lambda i, j: (layer, j, 0)),
                  pl.BlockSpec((tm, ne1, PEER_NKEYS), lambda i, j: (i, j, 0))],
        out_specs=pl.BlockSpec((tm, te), lambda i, j: (i, j)),
        out_shape=jax.ShapeDtypeStruct((t, e), BF16),
        compiler_params=_params("parallel", "parallel"),
    )(h, u, wts)


def kernel(x, c, w_in, w_branch, w_out, ada_w, ada_b, ada_table, hg_lb_logits, rw_mu, rw_w0, rw_w2, rw_a0, rw_a2, rw_g2, rw_k_k, rw_k_a, rw_r_k, rw_ln_w, rw_ln_b, rw_v0, rw_v1, rw_v2, fox_fb, peer_wq, peer_keys, peer_u, peer_v, final_norm_w):
    b, t, d = x.shape
    assert b == 1 and d == D_MODEL
    assert w_in.shape[2] == COL_GATE + N_BRANCH * d and COL_RW % 256 == 0 and COL_FOX % 256 == 0 and COL_FF % 256 == 0
    depth = w_in.shape[0]
    x = x.reshape(t, d)

    c8 = jnp.broadcast_to(c, (8, d))
    mods = _matmul(c8, ada_w, n_out=N_MOD * d, tm=8, tn=512, out_dtype=F32, pre="silu",
                   bias=ada_b.reshape(1, -1))[0].reshape(N_MOD, d)
    lbs = _hg_lower_bounds(hg_lb_logits)
    ret_tables = _retention_tables(t)
    w_gate = w_in[:, :, COL_GATE:].astype(BF16)
    w_ff = w_in[:, :, COL_FF:COL_GATE]
    u_bf, v_bf = peer_u.astype(BF16), peer_v.astype(BF16)
    zpad = lambda z, axis, n: jnp.pad(z, [(0, n - z.shape[i]) if i == axis else (0, 0) for i in range(z.ndim)])

    v_first = None
    for l in range(depth):
        m = mods + ada_table[l]
        sh1, sc1, g1, sh2, sc2, g2 = [m[i:i + 1] for i in range(N_MOD)]
        h = _rms_mod(x, 1.0 + sc1, sh1, BF16)
        p_rh = _matmul(h, w_in, layer=l, n_out=COL_RW, tm=1024, tn=256, out_dtype=F32)
        p_rw = _matmul(h, w_in, layer=l, col_block0=COL_RW // 256, n_out=RW_COLS, tm=1024, tn=256, out_dtype=F32)
        p_fox = _matmul(h, w_in, layer=l, col_block0=COL_FOX // 256, n_out=3 * 1024, tm=1024, tn=256, out_dtype=BF16)
        p_gate = _matmul(h, w_gate, layer=l, n_out=N_BRANCH * d, tm=1024, tn=256, out_dtype=F32)
        ya = _retention(p_rh, ret_tables)
        yb = _hgrn2(p_rh, lbs[l:l + 1])
        w2p = jnp.concatenate([rw_w2[l], jnp.zeros_like(rw_a2[l])], axis=0)
        a2p = jnp.concatenate([jnp.zeros_like(rw_w2[l]), rw_a2[l]], axis=0)
        vres = None
        if l > 0:
            vres = (v_first, rw_v0[l - 1][None], zpad(rw_v1[l - 1], 1, 128), zpad(rw_v2[l - 1], 0, 128))
        prep = _rw_prep(p_rw, rw_mu[l][None], rw_w0[l][None], w2p, rw_a0[l][None], a2p, rw_g2[l], vres)
        if l == 0:
            v_first = prep[3]
        yc = _rwkv7(prep, rw_k_k[l][None], rw_k_a[l][None], rw_r_k[l].reshape(1, RW_W), rw_ln_w[l][None], rw_ln_b[l][None])
        cum_row, cum_col = _fox_cum(h, w_ff[l], fox_fb[l])
        yd = _fox_attention(p_fox, cum_row, cum_col)
        merged = _merge([ya, yb, yc, yd], w_branch, l, p_gate)
        x = _matmul(merged, w_out, layer=l, n_out=d, tm=1024, tn=256, out_dtype=F32, res=x, gate=g1)
        h2 = _rms_mod(x, 1.0 + sc2, sh2, BF16)
        q = _matmul(h2, peer_wq, layer=l, n_out=PEER_HEADS * PEER_DKEY, tm=1024, tn=256, out_dtype=F32)
        ea, eb, eg = _peer_topk(q, peer_keys[l])
        wts = _peer_weights(ea, eb, eg)
        act = _peer_act(h2, u_bf, l, wts)
        x = _matmul_acc_res(act, v_bf, l, x, g2, tm=1024, tn=1024, tk=2048)
    out = _rms_mod(x, final_norm_w[None], jnp.zeros((1, d), F32), F32)
    return out.reshape(b, t, d)
```

```python
import functools

import jax
import jax.numpy as jnp
import numpy as np
from jax import lax
from jax.experimental import pallas as pl
from jax.experimental.pallas import tpu as pltpu

F32, BF16, I32 = jnp.float32, jnp.bfloat16, jnp.int32

D_MODEL = 4096
NORM_EPS = 1e-6
N_MOD = 6
RET_HEADS, RET_DK, RET_CHUNK, ROPE_BASE = 8, 128, 128, 10000.0
HG_HEADS, HG_DK = 8, 128
HG_SUB = 16
RW_HEADS, RW_HD = 16, 64
RW_W = RW_HEADS * RW_HD
RW_CHUNK = 64
RW_LN_EPS = 64e-5
FOX_HEADS, FOX_HD = 8, 128
N_BRANCH, BRANCH_W = 4, 1024
PEER_HEADS, PEER_NKEYS, PEER_DKEY, PEER_TOPK = 8, 128, 256, 16
PEER_EXPERTS = PEER_NKEYS * PEER_NKEYS
COL_RW = 8192
RW_COLS = 3 * RW_W + 64 + 64 + 128
COL_FOX = COL_RW + RW_COLS
COL_FF = COL_FOX + 3 * FOX_HEADS * FOX_HD
COL_GATE = COL_FF + FOX_HEADS

V7X_VMEM_LIMIT_BYTES = 56 * 1024 * 1024

_NN = (((1,), (0,)), ((), ()))
_NT = (((1,), (1,)), ((), ()))
_TN = (((0,), (0,)), ((), ()))


def _dot(a, b, dims=_NN):
    return lax.dot_general(a.astype(BF16), b.astype(BF16), dims, preferred_element_type=F32)


_BNN = (((2,), (1,)), ((0,), (0,)))
_BNT = (((2,), (2,)), ((0,), (0,)))
_BTN = (((1,), (1,)), ((0,), (0,)))


def _block_tril(size, block):
    ti = lax.broadcasted_iota(I32, (size, size), 0)
    si = lax.broadcasted_iota(I32, (size, size), 1)
    return ((ti >= si) & (ti // block == si // block)).astype(F32)


def _dot_sel(sel, x, x_first=False):
    hi = x.astype(BF16)
    r1 = x - hi.astype(F32)
    mid = r1.astype(BF16)
    lo = (r1 - mid.astype(F32)).astype(BF16)
    s = sel.astype(BF16)
    out = None
    for part in (hi, mid, lo):
        ab = (part, s) if x_first else (s, part)
        term = lax.dot_general(*ab, _NN, preferred_element_type=F32)
        out = term if out is None else out + term
    return out


def _params(*sem):
    return pltpu.CompilerParams(dimension_semantics=sem, vmem_limit_bytes=V7X_VMEM_LIMIT_BYTES)


def _mm_kernel(*refs, pre, has_bias, has_res):
    a_ref, w_ref = refs[0], refs[1]
    k = 2
    a = a_ref[...]
    if pre == "silu":
        a = a * jax.nn.sigmoid(a)
    y = _dot(a, w_ref[...])
    if has_bias:
        y = y + refs[k][...]
        k += 1
    if has_res:
        y = refs[k][...] + refs[k + 1][...] * y
        k += 2
    refs[k][...] = y.astype(refs[k].dtype)


def _matmul(a, w, *, name, n_out, tm, tn, out_dtype, layer=None, col_block0=0, pre=None, bias=None, res=None, gate=None):
    m, kdim = a.shape
    assert m % tm == 0
    grid = (m // tm, pl.cdiv(n_out, tn))
    if layer is None:
        w_spec = pl.BlockSpec((kdim, tn), lambda i, j: (0, j + col_block0))
    else:
        w_spec = pl.BlockSpec((None, kdim, tn), lambda i, j: (layer, 0, j + col_block0))
    in_specs = [pl.BlockSpec((tm, kdim), lambda i, j: (i, 0)), w_spec]
    args = [a, w]
    if bias is not None:
        in_specs.append(pl.BlockSpec((1, tn), lambda i, j: (0, j)))
        args.append(bias)
    if res is not None:
        in_specs += [pl.BlockSpec((tm, tn), lambda i, j: (i, j)), pl.BlockSpec((1, tn), lambda i, j: (0, j))]
        args += [res, gate]
    return pl.pallas_call(
        functools.partial(_mm_kernel, pre=pre, has_bias=bias is not None, has_res=res is not None),
        grid=grid,
        in_specs=in_specs,
        out_specs=pl.BlockSpec((tm, tn), lambda i, j: (i, j)),
        out_shape=jax.ShapeDtypeStruct((m, n_out), out_dtype),
        compiler_params=_params("parallel", "parallel"),
        name=name,
    )(*args)


def _mm_acc_kernel(a_ref, w_ref, x_ref, g_ref, o_ref, acc_ref):
    @pl.when(pl.program_id(2) == 0)
    def _():
        acc_ref[...] = jnp.zeros_like(acc_ref)

    acc_ref[...] += _dot(a_ref[...], w_ref[...])

    @pl.when(pl.program_id(2) == pl.num_programs(2) - 1)
    def _():
        o_ref[...] = x_ref[...] + g_ref[...] * acc_ref[...]


def _matmul_acc_res(a, w, layer, res, gate, *, tm, tn, tk):
    m, kdim = a.shape
    n = w.shape[2]
    return pl.pallas_call(
        _mm_acc_kernel,
        grid=(m // tm, n // tn, kdim // tk),
        in_specs=[
            pl.BlockSpec((tm, tk), lambda i, j, k: (i, k)),
            pl.BlockSpec((None, tk, tn), lambda i, j, k: (layer, k, j)),
            pl.BlockSpec((tm, tn), lambda i, j, k: (i, j)),
            pl.BlockSpec((1, tn), lambda i, j, k: (0, j)),
        ],
        out_specs=pl.BlockSpec((tm, tn), lambda i, j, k: (i, j)),
        out_shape=jax.ShapeDtypeStruct((m, n), F32),
        scratch_shapes=[pltpu.VMEM((tm, tn), F32)],
        compiler_params=_params("parallel", "parallel", "arbitrary"),
        name="peer_out",
    )(a, w, res, gate)


def _norm_kernel(x_ref, mul_ref, add_ref, o_ref):
    x = x_ref[...]
    ms = jnp.mean(x * x, axis=-1, keepdims=True)
    o_ref[...] = (x * lax.rsqrt(ms + NORM_EPS) * mul_ref[...] + add_ref[...]).astype(o_ref.dtype)


def _rms_mod(x, mul, add, out_dtype, tm=256):
    m, d = x.shape
    return pl.pallas_call(
        _norm_kernel,
        grid=(m // tm,),
        in_specs=[pl.BlockSpec((tm, d), lambda i: (i, 0)), pl.BlockSpec((1, d), lambda i: (0, 0)),
                  pl.BlockSpec((1, d), lambda i: (0, 0))],
        out_specs=pl.BlockSpec((tm, d), lambda i: (i, 0)),
        out_shape=jax.ShapeDtypeStruct((m, d), out_dtype),
        compiler_params=_params("parallel"),
        name="rms_mod",
    )(x, mul, add)


def _hg_lb_kernel(lg_ref, o_ref):
    lg = lg_ref[...]
    e = jnp.exp(lg - jnp.max(lg, axis=0, keepdims=True))
    p = e / jnp.sum(e, axis=0, keepdims=True)
    rows = [jnp.zeros_like(p[0:1])]
    for l in range(1, lg.shape[0]):
        rows.append(rows[-1] + p[l:l + 1])
    o_ref[...] = jnp.concatenate(rows, axis=0)


def _hg_lower_bounds(logits):
    return pl.pallas_call(_hg_lb_kernel, out_shape=jax.ShapeDtypeStruct(logits.shape, F32), name="hg_lb")(logits)


def _ret_kernel(q_ref, k_ref, v_ref, g_ref, cos_ref, sin_ref, dmat_ref, xi_ref, zeta_ref, gam_ref, o_ref, st_ref, *, tb):
    c = RET_CHUNK

    @pl.when(pl.program_id(1) == 0)
    def _():
        st_ref[...] = jnp.zeros_like(st_ref)

    dmat, xi, zeta, gam = dmat_ref[...], xi_ref[...], zeta_ref[...], gam_ref[...]
    for ci in range(tb // c):
        sl = slice(ci * c, (ci + 1) * c)
        q, k, v = q_ref[sl, :], k_ref[sl, :], v_ref[sl, :]
        cos, sin = cos_ref[sl, :], sin_ref[sl, :]
        half = RET_DK // 2
        qr = q * cos + pltpu.roll(q, half, 1) * sin
        kr = (k * cos + pltpu.roll(k, half, 1) * sin) * (RET_DK ** -0.5)
        scores = _dot(qr, kr, _NT) * dmat
        o = _dot(scores, v)
        st = st_ref[...]
        o = o + _dot(qr * xi, st)
        st_ref[...] = st * gam + _dot(kr * zeta, v, _TN)
        ms = jnp.mean(o * o, axis=-1, keepdims=True)
        g = g_ref[sl, :]
        y = o * lax.rsqrt(ms + NORM_EPS) * (g * jax.nn.sigmoid(g))
        o_ref[sl, :] = y.astype(o_ref.dtype)


def _retention(proj, tables, tb=512):
    t = proj.shape[0]
    h = RET_HEADS
    cos, sin, dmat, xi, zeta, gam = tables
    c = RET_CHUNK
    col = lambda g: pl.BlockSpec((tb, 128), lambda hh, n, g=g: (n, g * h + hh))
    tab = pl.BlockSpec((tb, 128), lambda hh, n: (n, 0))
    return pl.pallas_call(
        functools.partial(_ret_kernel, tb=tb),
        grid=(h, t // tb),
        in_specs=[col(0), col(1), col(2), col(3), tab, tab,
                  pl.BlockSpec((None, c, c), lambda hh, n: (hh, 0, 0)),
                  pl.BlockSpec((None, c, 128), lambda hh, n: (hh, 0, 0)),
                  pl.BlockSpec((None, c, 128), lambda hh, n: (hh, 0, 0)),
                  pl.BlockSpec((None, 1, 128), lambda hh, n: (hh, 0, 0))],
        out_specs=pl.BlockSpec((tb, 128), lambda hh, n: (n, hh)),
        out_shape=jax.ShapeDtypeStruct((t, h * 128), BF16),
        scratch_shapes=[pltpu.VMEM((RET_DK, 128), F32)],
        compiler_params=_params("parallel", "arbitrary"),
        name="retention",
    )(proj, proj, proj, proj, cos, sin, dmat, xi, zeta, gam)


def _retention_tables(t):
    half = RET_DK // 2
    inv = ROPE_BASE ** (-jnp.arange(half, dtype=F32) / half)
    ang = jnp.arange(t, dtype=F32)[:, None] * inv[None, :]
    cos, sin = jnp.cos(ang), jnp.sin(ang)
    cos2 = jnp.concatenate([cos, cos], axis=-1)
    sin2 = jnp.concatenate([-sin, sin], axis=-1)
    c = RET_CHUNK
    log_g = jnp.log(1.0 - 2.0 ** (-5.0 - jnp.arange(RET_HEADS, dtype=F32)))
    pos = jnp.arange(c, dtype=F32)
    rel = pos[:, None] - pos[None, :]
    dmat = jnp.where(rel >= 0, jnp.exp(log_g[:, None, None] * jnp.maximum(rel, 0.0)), 0.0)
    xi = jnp.broadcast_to(jnp.exp(log_g[:, None] * (pos + 1.0))[:, :, None], (RET_HEADS, c, 128))
    zeta = jnp.broadcast_to(jnp.exp(log_g[:, None] * (c - 1.0 - pos))[:, :, None], (RET_HEADS, c, 128))
    gam = jnp.broadcast_to(jnp.exp(log_g * c)[:, None, None], (RET_HEADS, 1, 128))
    return cos2, sin2, dmat, xi, zeta, gam


def _hg_kernel(q_ref, f_ref, v_ref, g_ref, lb_ref, o_ref, st_ref, *, tb):
    n = HG_SUB

    @pl.when(pl.program_id(1) == 0)
    def _():
        st_ref[...] = jnp.zeros_like(st_ref)

    nb = tb // n
    lb = lb_ref[...]
    fg = lb + (1.0 - lb) * jax.nn.sigmoid(f_ref[...])
    kk = (1.0 - fg).reshape(nb, n, 128)
    cum = _dot_sel(_block_tril(tb, n), jnp.log(fg)).reshape(nb, n, 128)
    q, v = q_ref[...].reshape(nb, n, 128), v_ref[...].reshape(nb, n, 128)
    row = lax.broadcasted_iota(I32, (nb, n, 128), 1)
    o = jnp.zeros((nb, n, 128), F32)
    for s in range(n):
        d = jnp.where(row >= s, cum - cum[:, s:s + 1, :], -jnp.inf)
        col = jnp.sum(q * jnp.exp(d) * kk[:, s:s + 1, :], axis=-1, keepdims=True)
        o = o + col * v[:, s:s + 1, :]
    last = cum[:, n - 1:n, :]
    upd = lax.dot_general(v.astype(BF16), (kk * jnp.exp(last - cum)).astype(BF16), _BTN, preferred_element_type=F32)
    dec = jnp.exp(last)
    st = st_ref[...]
    entering = []
    for j in range(nb):
        entering.append(st)
        st = st * dec[j] + upd[j]
    st_ref[...] = st
    o = o + lax.dot_general((q * jnp.exp(cum)).astype(BF16), jnp.stack(entering).astype(BF16), _BNT,
                            preferred_element_type=F32)
    o = o.reshape(tb, 128)
    ms = jnp.mean(o * o, axis=-1, keepdims=True)
    g = g_ref[...]
    o_ref[...] = (o * lax.rsqrt(ms + NORM_EPS) * (g * jax.nn.sigmoid(g))).astype(o_ref.dtype)


def _hgrn2(proj, lb, tb=256):
    t = proj.shape[0]
    h = HG_HEADS
    col = lambda g: pl.BlockSpec((tb, 128), lambda hh, n, g=g: (n, (4 + g) * h + hh))
    return pl.pallas_call(
        functools.partial(_hg_kernel, tb=tb),
        grid=(h, t // tb),
        in_specs=[col(0), col(1), col(2), col(3), pl.BlockSpec((1, 128), lambda hh, n: (0, hh))],
        out_specs=pl.BlockSpec((tb, 128), lambda hh, n: (n, hh)),
        out_shape=jax.ShapeDtypeStruct((t, h * 128), BF16),
        scratch_shapes=[pltpu.VMEM((128, HG_DK), F32)],
        compiler_params=_params("parallel", "arbitrary"),
        name="hgrn2",
    )(proj, proj, proj, proj, lb)


def _rw_prep_kernel(*refs, has_vres):
    if has_vres:
        (cur_ref, prv_ref, mu_ref, w0_ref, w2_ref, a0_ref, a2_ref, g2_ref, vf_ref, v0_ref, v1_ref, v2_ref,
         r_ref, ld_ref, k_ref, v_ref, a_ref, g_ref) = refs
    else:
        (cur_ref, prv_ref, mu_ref, w0_ref, w2_ref, a0_ref, a2_ref, g2_ref,
         r_ref, ld_ref, k_ref, v_ref, a_ref, g_ref) = refs
    cur = cur_ref[...]
    tt = cur.shape[0]
    prev_row = jnp.where(pl.program_id(0) == 0, 0.0, prv_ref[7:8, :])
    row = lax.broadcasted_iota(I32, cur.shape, 0)
    shifted = jnp.where(row == 0, prev_row, pltpu.roll(cur, 1, 0))
    cols = cur + (shifted - cur) * mu_ref[...]
    w = RW_W
    r, k, v = cols[:, 0:w], cols[:, w:2 * w], cols[:, 2 * w:3 * w]
    wa = cols[:, 3 * w:3 * w + 128]
    lane = lax.broadcasted_iota(I32, wa.shape, 1)
    wa = jnp.where(lane < 64, jnp.tanh(wa), wa)
    gl = cols[:, 3 * w + 128:3 * w + 256]
    wlog = -jax.nn.softplus(-(w0_ref[...] + _dot(wa, w2_ref[...]))) - 0.5
    a = jax.nn.sigmoid(a0_ref[...] + _dot(wa, a2_ref[...]))
    if has_vres:
        mix = jax.nn.sigmoid(v0_ref[...] + _dot(_dot(v, v1_ref[...]), v2_ref[...]))
        v = v + (vf_ref[...] - v) * mix
    r_ref[...] = r
    ld_ref[...] = -jnp.exp(wlog)
    k_ref[...] = k
    v_ref[...] = v
    a_ref[...] = a
    g_ref[...] = _dot(jax.nn.sigmoid(gl), g2_ref[...])


def _rw_prep(cols, mu, w0, w2p, a0, a2p, g2, vres, tt=256):
    t, nc = cols.shape
    w = RW_W
    full = lambda shape: pl.BlockSpec(shape, lambda i: (0,) * len(shape))
    in_specs = [pl.BlockSpec((tt, nc), lambda i: (i, 0)),
                pl.BlockSpec((8, nc), lambda i: (jnp.maximum(i * (tt // 8) - 1, 0), 0)),
                full((1, nc)), full((1, w)), full((128, w)), full((1, w)), full((128, w)), full((128, w))]
    args = [cols, cols, mu, w0, w2p, a0, a2p, g2]
    if vres is not None:
        v_first, v0, v1p, v2p = vres
        in_specs += [pl.BlockSpec((tt, w), lambda i: (i, 0)), full((1, w)), full((w, 128)), full((128, w))]
        args += [v_first, v0, v1p, v2p]
    out = jax.ShapeDtypeStruct((t, w), F32)
    return pl.pallas_call(
        functools.partial(_rw_prep_kernel, has_vres=vres is not None),
        grid=(t // tt,),
        in_specs=in_specs,
        out_specs=[pl.BlockSpec((tt, w), lambda i: (i, 0))] * 6,
        out_shape=[out] * 6,
        compiler_params=_params("parallel"),
        name="rwkv_prep",
    )(*args)


def _rw_level_masks(c):
    ti = lax.broadcasted_iota(I32, (c, c), 0)
    si = lax.broadcasted_iota(I32, (c, c), 1)
    masks = []
    n = 1
    while n < c:
        same = (ti // (2 * n)) == (si // (2 * n))
        masks.append(same & ((ti // n) % 2 == 1) & ((si // n) % 2 == 0))
        n *= 2
    return ti, si, masks


def _bdot(a, b, dims=_BNN):
    return lax.dot_general(a.astype(BF16), b.astype(BF16), dims, preferred_element_type=F32)


def _rw_kernel(r_ref, ld_ref, k_ref, v_ref, a_ref, g_ref, kk_ref, ka_ref, rk_ref, lnw_ref, lnb_ref, o_ref, st_ref, *, tb):
    c, nh, hd = RW_CHUNK, RW_HEADS, RW_HD
    nc = tb // c

    @pl.when(pl.program_id(0) == 0)
    def _():
        st_ref[...] = jnp.zeros_like(st_ref)

    def split(x):
        return jnp.stack([x[ci * c:(ci + 1) * c, h * hd:(h + 1) * hd] for ci in range(nc) for h in range(nh)])

    def split_row(ref):
        x = ref[...]
        return jnp.stack([x[:, h * hd:(h + 1) * hd] for _ in range(nc) for h in range(nh)])

    ld2 = ld_ref[...]
    cw2 = _dot_sel(_block_tril(tb, c), ld2)
    r, ld, cw, k, v, a = split(r_ref[...]), split(ld2), split(cw2), split(k_ref[...]), split(v_ref[...]), split(a_ref[...])
    kk_w, ka_w, rk_w, lnw, lnb = (split_row(p) for p in (kk_ref, ka_ref, rk_ref, lnw_ref, lnb_ref))
    ti, si, masks = _rw_level_masks(c)
    tril, strict = (ti >= si)[None], (ti > si)[None]

    wc = jnp.exp(cw)
    wprev = jnp.exp(cw - ld)
    winv = jnp.exp(-cw)
    wc_last = wc[:, c - 1:c, :]
    wtail = wc_last * winv
    kkf = k * kk_w
    kk = kkf / jnp.maximum(jnp.sqrt(jnp.sum(kkf * kkf, axis=-1, keepdims=True)), 1e-12)
    k2 = k * (1.0 + (a - 1.0) * ka_w)
    bv = kk * a
    at = -kk * wprev
    rt = r * wc
    ar = jnp.concatenate([at, rt], axis=1)
    gb = _bdot(ar, bv * winv, _BNT)
    gk = _bdot(ar, k2 * winv, _BNT)
    lab = jnp.where(strict, gb[:, :c], 0.0)
    lak = jnp.where(strict, gk[:, :c], 0.0)
    mrb = jnp.where(tril, gb[:, c:], 0.0)
    mrk = jnp.where(tril, gk[:, c:], 0.0)
    x = jnp.where((ti == si)[None], 1.0, 0.0) + jnp.where(masks[0][None], lab, 0.0)
    for m in masks[1:]:
        x = x + _bdot(_bdot(x, jnp.where(m[None], lab, 0.0)), x)
    lv = _bdot(jnp.concatenate([lak, mrk], axis=1), v)
    p = _bdot(x, at)
    q = _bdot(x, lv[:, :c])
    r2 = rt + _bdot(mrb, p)
    y0 = _bdot(mrb, q) + lv[:, c:]
    bh = bv * wtail
    g = _bdot(p, bh, _BTN)
    s_add = _bdot(q, bh, _BTN) + _bdot(v, k2 * wtail, _BTN)
    s = st_ref[...]
    entering = []
    for ci in range(nc):
        b0, b1 = ci * nh, (ci + 1) * nh
        entering.append(s)
        s = s * wc_last[b0:b1] + _bdot(s, g[b0:b1]) + s_add[b0:b1]
    st_ref[...] = s
    y = _bdot(r2, jnp.concatenate(entering, axis=0), _BNT) + y0
    mean = jnp.mean(y, axis=-1, keepdims=True)
    var = jnp.mean(jnp.square(y - mean), axis=-1, keepdims=True)
    yn = (y - mean) * lax.rsqrt(var + RW_LN_EPS) * lnw + lnb
    out = yn + jnp.sum(r * k2 * rk_w, axis=-1, keepdims=True) * v
    for ci in range(nc):
        rows = jnp.concatenate([out[ci * nh + h] for h in range(nh)], axis=1)
        o_ref[ci * c:(ci + 1) * c, :] = (rows * g_ref[ci * c:(ci + 1) * c, :]).astype(o_ref.dtype)


def _rwkv7(prep, k_k, k_a, r_k, ln_w, ln_b, tb=128):
    r, ld, k, v, a, g = prep
    t = r.shape[0]
    col = pl.BlockSpec((tb, RW_W), lambda n: (n, 0))
    par = pl.BlockSpec((1, RW_W), lambda n: (0, 0))
    return pl.pallas_call(
        functools.partial(_rw_kernel, tb=tb),
        grid=(t // tb,),
        in_specs=[col] * 6 + [par] * 5,
        out_specs=col,
        out_shape=jax.ShapeDtypeStruct((t, RW_W), BF16),
        scratch_shapes=[pltpu.VMEM((RW_HEADS, RW_HD, RW_HD), F32)],
        compiler_params=_params("arbitrary"),
        name="rwkv7",
    )(r, ld, k, v, a, g, k_k, k_a, r_k, ln_w, ln_b)


def _fox_cum_kernel(h_ref, wt_ref, w_ref, fbc_ref, fbr_ref, row_ref, col_ref, crow_ref, ccol_ref):
    @pl.when(pl.program_id(0) == 0)
    def _():
        crow_ref[...] = jnp.zeros_like(crow_ref)
        ccol_ref[...] = jnp.zeros_like(ccol_ref)

    h = h_ref[...]
    tt = h.shape[0]
    ti = lax.broadcasted_iota(I32, (tt, tt), 0)
    si = lax.broadcasted_iota(I32, (tt, tt), 1)
    lf_r = jax.nn.log_sigmoid(_dot(wt_ref[...], h, _NT) + fbc_ref[...])
    cum_r = _dot_sel((ti <= si).astype(F32), lf_r, x_first=True) + crow_ref[:, 0:1]
    row_ref[...] = cum_r
    crow_ref[...] = jnp.broadcast_to(cum_r[:, tt - 1:tt], crow_ref.shape)
    lf_c = jax.nn.log_sigmoid(_dot(h, w_ref[...]) + fbr_ref[...])
    cum_c = _dot_sel((ti >= si).astype(F32), lf_c) + ccol_ref[0:1, :]
    col_ref[...] = cum_c
    ccol_ref[...] = jnp.broadcast_to(cum_c[tt - 1:tt, :], ccol_ref.shape)


def _fox_cum(h, w_ff, fb, tt=512):
    t, d = h.shape
    nh = FOX_HEADS
    full = lambda shape: pl.BlockSpec(shape, lambda i: (0,) * len(shape))
    return pl.pallas_call(
        _fox_cum_kernel,
        grid=(t // tt,),
        in_specs=[pl.BlockSpec((tt, d), lambda i: (i, 0)), full((nh, d)), full((d, nh)), full((nh, 1)), full((1, nh))],
        out_specs=[pl.BlockSpec((nh, tt), lambda i: (0, i)), pl.BlockSpec((tt, nh), lambda i: (i, 0))],
        out_shape=[jax.ShapeDtypeStruct((nh, t), F32), jax.ShapeDtypeStruct((t, nh), F32)],
        scratch_shapes=[pltpu.VMEM((nh, 128), F32), pltpu.VMEM((8, nh), F32)],
        compiler_params=_params("arbitrary"),
        name="fox_cum",
    )(h, w_ff.T, w_ff, fb.reshape(nh, 1), fb.reshape(1, nh))


def _fox_kernel(q_ref, k_ref, v_ref, cq_ref, ck_ref, o_ref, m_ref, l_ref, acc_ref, *, tq):
    hh, qi = pl.program_id(0), pl.program_id(1)
    nk = k_ref.shape[0] // tq
    q = q_ref[...]
    lane = lax.broadcasted_iota(I32, cq_ref.shape, 1)
    cq = jnp.sum(jnp.where(lane == hh, cq_ref[...], 0.0), axis=-1, keepdims=True)
    m_ref[...] = jnp.full_like(m_ref, -jnp.inf)
    l_ref[...] = jnp.zeros_like(l_ref)
    acc_ref[...] = jnp.zeros_like(acc_ref)

    def block(ki, diagonal):
        rows = pl.ds(pl.multiple_of(ki * tq, tq), tq)
        s = _dot(q, k_ref[rows, :], _NT) * (FOX_HD ** -0.5)
        s = s + cq - ck_ref[pl.ds(hh * nk + ki, 1), :]
        if diagonal:
            s = jnp.where(lax.broadcasted_iota(I32, s.shape, 1) <= lax.broadcasted_iota(I32, s.shape, 0), s, -jnp.inf)
        m_old = m_ref[...]
        m_new = jnp.maximum(m_old, jnp.max(s, axis=-1, keepdims=True))
        alpha = jnp.exp(m_old - m_new)
        p = jnp.exp(s - m_new)
        l_ref[...] = alpha * l_ref[...] + jnp.sum(p, axis=-1, keepdims=True)
        acc_ref[...] = alpha * acc_ref[...] + _dot(p, v_ref[rows, :])
        m_ref[...] = m_new

    def body(ki, carry):
        block(ki, False)
        return carry

    lax.fori_loop(0, qi, body, 0)
    block(qi, True)
    o_ref[...] = (acc_ref[...] / l_ref[...]).astype(o_ref.dtype)


def _fox_attention(qkv, cum_row, cum_col, tq=512):
    t = qkv.shape[0]
    h = FOX_HEADS
    nq = t // tq
    return pl.pallas_call(
        functools.partial(_fox_kernel, tq=tq),
        grid=(h, nq),
        in_specs=[pl.BlockSpec((tq, 128), lambda hh, qi: (qi, hh)),
                  pl.BlockSpec((t, 128), lambda hh, qi: (0, h + hh)),
                  pl.BlockSpec((t, 128), lambda hh, qi: (0, 2 * h + hh)),
                  pl.BlockSpec((tq, h), lambda hh, qi: (qi, 0)),
                  pl.BlockSpec((h * nq, tq), lambda hh, qi: (0, 0))],
        out_specs=pl.BlockSpec((tq, 128), lambda hh, qi: (qi, hh)),
        out_shape=jax.ShapeDtypeStruct((t, h * 128), BF16),
        scratch_shapes=[pltpu.VMEM((tq, 1), F32), pltpu.VMEM((tq, 1), F32), pltpu.VMEM((tq, 128), F32)],
        compiler_params=_params("parallel", "arbitrary"),
        name="fox_attention",
    )(qkv, qkv, qkv, cum_col, cum_row.reshape(h * nq, tq))


def _merge_kernel(*refs):
    y_refs, w_refs, g_refs, o_ref = refs[0:4], refs[4:8], refs[8:12], refs[12]
    acc = None
    for n in range(N_BRANCH):
        term = jax.nn.sigmoid(g_refs[n][...]) * _dot(y_refs[n][...], w_refs[n][...])
        acc = term if acc is None else acc + term
    o_ref[...] = acc.astype(o_ref.dtype)


def _merge(branches, w_branch, layer, gates, tm=1024, tn=512):
    t = branches[0].shape[0]
    d = D_MODEL
    nb = d // tn
    y_spec = pl.BlockSpec((tm, BRANCH_W), lambda i, j: (i, 0))
    w_specs = [pl.BlockSpec((None, None, BRANCH_W, tn), lambda i, j, n=n: (layer, n, 0, j)) for n in range(N_BRANCH)]
    g_specs = [pl.BlockSpec((tm, tn), lambda i, j, n=n: (i, n * nb + j)) for n in range(N_BRANCH)]
    return pl.pallas_call(
        _merge_kernel,
        grid=(t // tm, nb),
        in_specs=[y_spec] * 4 + w_specs + g_specs,
        out_specs=pl.BlockSpec((tm, tn), lambda i, j: (i, j)),
        out_shape=jax.ShapeDtypeStruct((t, d), BF16),
        compiler_params=_params("parallel", "parallel"),
        name="merge",
    )(*branches, *([w_branch] * N_BRANCH), *([gates] * N_BRANCH))


def _topk_rows(s, k):
    n = s.shape[0]
    idx_iota = lax.broadcasted_iota(I32, s.shape, 0)
    vals, idxs = [], []
    for _ in range(k):
        m = jnp.max(s, axis=0, keepdims=True)
        idx = jnp.min(jnp.where(s == m, idx_iota, n), axis=0, keepdims=True)
        vals.append(m)
        idxs.append(idx)
        s = jnp.where(idx_iota == idx, -jnp.inf, s)
    return jnp.concatenate(vals, axis=0), jnp.concatenate(idxs, axis=0)


def _peer_topk_kernel(q_ref, keys_ref, a_ref, b_ref, g_ref):
    kq = PEER_TOPK
    half = PEER_DKEY // 2
    a_rows, b_rows, g_rows = [], [], []
    for hh in range(PEER_HEADS):
        top = []
        for p in range(2):
            qs = q_ref[:, (2 * hh + p) * half:(2 * hh + p + 1) * half]
            s = _dot(keys_ref[hh, p], qs, _NT)
            top.append(_topk_rows(s, kq))
        (v1, i1), (v2, i2) = top
        cand = jnp.concatenate([v1[r:r + 1, :] + v2 for r in range(kq)], axis=0)
        best, j = _topk_rows(cand, kq)
        r1, r2 = j // kq, j % kq
        e1 = jnp.zeros_like(j)
        e2 = jnp.zeros_like(j)
        for r in range(kq):
            e1 = e1 + jnp.where(r1 == r, i1[r:r + 1, :], 0)
            e2 = e2 + jnp.where(r2 == r, i2[r:r + 1, :], 0)
        ex = jnp.exp(best - best[0:1, :])
        a_rows.append(e1)
        b_rows.append(e2)
        g_rows.append(ex / jnp.sum(ex, axis=0, keepdims=True))
    a_ref[...] = jnp.concatenate(a_rows, axis=0).T
    b_ref[...] = jnp.concatenate(b_rows, axis=0).T
    g_ref[...] = jnp.concatenate(g_rows, axis=0).T


def _peer_topk(q, keys, tt=256):
    t = q.shape[0]
    n = PEER_HEADS * PEER_TOPK
    spec = pl.BlockSpec((tt, n), lambda i: (i, 0))
    return pl.pallas_call(
        _peer_topk_kernel,
        grid=(t // tt,),
        in_specs=[pl.BlockSpec((tt, q.shape[1]), lambda i: (i, 0)),
                  pl.BlockSpec(keys.shape, lambda i: (0, 0, 0, 0))],
        out_specs=[spec, spec, spec],
        out_shape=[jax.ShapeDtypeStruct((t, n), I32), jax.ShapeDtypeStruct((t, n), I32), jax.ShapeDtypeStruct((t, n), F32)],
        compiler_params=_params("parallel"),
        name="peer_topk",
    )(q, keys)


def _peer_weights_kernel(a_ref, b_ref, g_ref, o_ref):
    tt = a_ref.shape[0]
    nk = PEER_NKEYS
    sub = lax.broadcasted_iota(I32, (nk, a_ref.shape[1]), 0)

    group = 8

    def body(i, carry):
        t0 = pl.multiple_of(i * group, group)
        lefts, rights = [], []
        for u in range(group):
            a = a_ref[pl.ds(t0 + u, 1), :]
            b = b_ref[pl.ds(t0 + u, 1), :]
            g = g_ref[pl.ds(t0 + u, 1), :]
            g_hi = g.astype(BF16).astype(F32)
            g_lo = g - g_hi
            lefts.append(jnp.concatenate([jnp.where(sub == a, g_hi, 0.0), jnp.where(sub == a, g_lo, 0.0)], axis=1))
            onehot = jnp.where(sub == b, 1.0, 0.0)
            rights.append(jnp.concatenate([onehot, onehot], axis=1))
        o_ref[pl.ds(t0, group)] = _bdot(jnp.stack(lefts), jnp.stack(rights), _BNT)
        return carry

    lax.fori_loop(0, tt // group, body, 0)


def _peer_weights(a, b, g, tt=64):
    t, n = a.shape
    spec = pl.BlockSpec((tt, n), lambda i: (i, 0))
    return pl.pallas_call(
        _peer_weights_kernel,
        grid=(t // tt,),
        in_specs=[spec, spec, spec],
        out_specs=pl.BlockSpec((tt, PEER_NKEYS, PEER_NKEYS), lambda i: (i, 0, 0)),
        out_shape=jax.ShapeDtypeStruct((t, PEER_NKEYS, PEER_NKEYS), F32),
        compiler_params=_params("parallel"),
        name="peer_weights",
    )(a, b, g)


def _peer_act_kernel(h_ref, u_ref, w_ref, o_ref, *, ne1):
    z = _dot(h_ref[...], u_ref[...], _NT)
    w = jnp.concatenate([w_ref[:, j, :] for j in range(ne1)], axis=1)
    gelu = 0.5 * z * (1.0 + lax.erf(z * (2.0 ** -0.5)))
    o_ref[...] = (gelu * w).astype(o_ref.dtype)


def _peer_act(h, u, layer, wts, tm=512, te=1024):
    t, d = h.shape
    e = u.shape[1]
    ne1 = te // PEER_NKEYS
    return pl.pallas_call(
        functools.partial(_peer_act_kernel, ne1=ne1),
        grid=(t // tm, e // te),
        in_specs=[pl.BlockSpec((tm, d), lambda i, j: (i, 0)),
                  pl.BlockSpec((None, te, d), lambda i, j: (layer, j, 0)),
                  pl.BlockSpec((tm, ne1, PEER_NKEYS), lambda i, j: (i, j, 0))],
        out_specs=pl.BlockSpec((tm, te), lambda i, j: (i, j)),
        out_shape=jax.ShapeDtypeStruct((t, e), BF16),
        compiler_params=_params("parallel", "parallel"),
        name="peer_act",
    )(h, u, wts)


def kernel(x, c, w_in, w_branch, w_out, ada_w, ada_b, ada_table, hg_lb_logits, rw_mu, rw_w0, rw_w2, rw_a0, rw_a2, rw_g2, rw_k_k, rw_k_a, rw_r_k, rw_ln_w, rw_ln_b, rw_v0, rw_v1, rw_v2, fox_fb, peer_wq, peer_keys, peer_u, peer_v, final_norm_w):
    b, t, d = x.shape
    assert b == 1 and d == D_MODEL
    assert w_in.shape[2] == COL_GATE + N_BRANCH * d and COL_RW % 256 == 0 and COL_FOX % 256 == 0 and COL_FF % 256 == 0
    depth = w_in.shape[0]
    x = x.reshape(t, d)

    c8 = jnp.broadcast_to(c, (8, d))
    mods = _matmul(c8, ada_w, name="ada_mods", n_out=N_MOD * d, tm=8, tn=512, out_dtype=F32, pre="silu",
                   bias=ada_b.reshape(1, -1))[0].reshape(N_MOD, d)
    lbs = _hg_lower_bounds(hg_lb_logits)
    ret_tables = _retention_tables(t)
    w_rh = w_in[:, :, :COL_RW].astype(BF16)
    w_rw = w_in[:, :, COL_RW:COL_FOX].astype(BF16)
    w_fox = w_in[:, :, COL_FOX:COL_FF].astype(BF16)
    w_ff = w_in[:, :, COL_FF:COL_GATE]
    w_gate = w_in[:, :, COL_GATE:].astype(BF16)
    u_bf, v_bf = peer_u.astype(BF16), peer_v.astype(BF16)
    zpad = lambda z, axis, n: jnp.pad(z, [(0, n - z.shape[i]) if i == axis else (0, 0) for i in range(z.ndim)])

    v_first = None
    for l in range(depth):
        m = mods + ada_table[l]
        sh1, sc1, g1, sh2, sc2, g2 = [m[i:i + 1] for i in range(N_MOD)]
        h = _rms_mod(x, 1.0 + sc1, sh1, BF16)
        p_rh = _matmul(h, w_rh, name="proj_ret_hg", layer=l, n_out=COL_RW, tm=1024, tn=512, out_dtype=F32)
        p_rw = _matmul(h, w_rw, name="proj_rwkv", layer=l, n_out=RW_COLS, tm=1024, tn=256, out_dtype=F32)
        p_fox = _matmul(h, w_fox, name="proj_fox", layer=l, n_out=COL_FF - COL_FOX, tm=1024, tn=512, out_dtype=BF16)
        p_gate = _matmul(h, w_gate, name="proj_gate", layer=l, n_out=N_BRANCH * d, tm=1024, tn=512, out_dtype=F32)
        ya = _retention(p_rh, ret_tables)
        yb = _hgrn2(p_rh, lbs[l:l + 1])
        w2p = jnp.concatenate([rw_w2[l], jnp.zeros_like(rw_a2[l])], axis=0)
        a2p = jnp.concatenate([jnp.zeros_like(rw_w2[l]), rw_a2[l]], axis=0)
        vres = None
        if l > 0:
            vres = (v_first, rw_v0[l - 1][None], zpad(rw_v1[l - 1], 1, 128), zpad(rw_v2[l - 1], 0, 128))
        prep = _rw_prep(p_rw, rw_mu[l][None], rw_w0[l][None], w2p, rw_a0[l][None], a2p, rw_g2[l], vres)
        if l == 0:
            v_first = prep[3]
        yc = _rwkv7(prep, rw_k_k[l][None], rw_k_a[l][None], rw_r_k[l].reshape(1, RW_W), rw_ln_w[l][None], rw_ln_b[l][None])
        cum_row, cum_col = _fox_cum(h, w_ff[l], fox_fb[l])
        yd = _fox_attention(p_fox, cum_row, cum_col)
        merged = _merge([ya, yb, yc, yd], w_branch, l, p_gate)
        x = _matmul(merged, w_out, name="out_proj", layer=l, n_out=d, tm=1024, tn=256, out_dtype=F32, res=x, gate=g1)
        h2 = _rms_mod(x, 1.0 + sc2, sh2, BF16)
        q = _matmul(h2, peer_wq, name="peer_query", layer=l, n_out=PEER_HEADS * PEER_DKEY, tm=1024, tn=256, out_dtype=F32)
        ea, eb, eg = _peer_topk(q, peer_keys[l])
        wts = _peer_weights(ea, eb, eg)
        act = _peer_act(h2, u_bf, l, wts)
        x = _matmul_acc_res(act, v_bf, l, x, g2, tm=1024, tn=1024, tk=2048)
    out = _rms_mod(x, final_norm_w[None], jnp.zeros((1, d), F32), F32)
    return out.reshape(b, t, d)
```

```python
import functools

import jax
import jax.numpy as jnp
import numpy as np
from jax import lax
from jax.experimental import pallas as pl
from jax.experimental.pallas import tpu as pltpu

F32, BF16, I32 = jnp.float32, jnp.bfloat16, jnp.int32

D_MODEL = 4096
NORM_EPS = 1e-6
N_MOD = 6
RET_HEADS, RET_DK, RET_CHUNK, ROPE_BASE = 8, 128, 128, 10000.0
HG_HEADS, HG_DK = 8, 128
HG_SUB = 16
RW_HEADS, RW_HD = 16, 64
RW_W = RW_HEADS * RW_HD
RW_CHUNK = 64
RW_LN_EPS = 64e-5
FOX_HEADS, FOX_HD = 8, 128
N_BRANCH, BRANCH_W = 4, 1024
PEER_HEADS, PEER_NKEYS, PEER_DKEY, PEER_TOPK = 8, 128, 256, 16
PEER_EXPERTS = PEER_NKEYS * PEER_NKEYS
COL_RW = 8192
RW_COLS = 3 * RW_W + 64 + 64 + 128
COL_FOX = COL_RW + RW_COLS
COL_FF = COL_FOX + 3 * FOX_HEADS * FOX_HD
COL_GATE = COL_FF + FOX_HEADS

V7X_VMEM_LIMIT_BYTES = 56 * 1024 * 1024
MM_TM, MM_TN = 2048, 512

_NN = (((1,), (0,)), ((), ()))
_NT = (((1,), (1,)), ((), ()))
_TN = (((0,), (0,)), ((), ()))


def _dot(a, b, dims=_NN):
    return lax.dot_general(a.astype(BF16), b.astype(BF16), dims, preferred_element_type=F32)


_BNN = (((2,), (1,)), ((0,), (0,)))
_BNT = (((2,), (2,)), ((0,), (0,)))
_BTN = (((1,), (1,)), ((0,), (0,)))


def _block_tril(size, block):
    ti = lax.broadcasted_iota(I32, (size, size), 0)
    si = lax.broadcasted_iota(I32, (size, size), 1)
    return ((ti >= si) & (ti // block == si // block)).astype(F32)


def _dot_sel(sel, x, x_first=False):
    hi = x.astype(BF16)
    r1 = x - hi.astype(F32)
    mid = r1.astype(BF16)
    lo = (r1 - mid.astype(F32)).astype(BF16)
    s = sel.astype(BF16)
    out = None
    for part in (hi, mid, lo):
        ab = (part, s) if x_first else (s, part)
        term = lax.dot_general(*ab, _NN, preferred_element_type=F32)
        out = term if out is None else out + term
    return out


def _params(*sem):
    return pltpu.CompilerParams(dimension_semantics=sem, vmem_limit_bytes=V7X_VMEM_LIMIT_BYTES)


def _mm_kernel(*refs, pre, has_bias, has_res, w_transposed):
    a_ref, w_ref = refs[0], refs[1]
    k = 2
    a = a_ref[...]
    if pre == "silu":
        a = a * jax.nn.sigmoid(a)
    y = _dot(a, w_ref[0], _NT) if w_transposed else _dot(a, w_ref[...])
    if has_bias:
        y = y + refs[k][...]
        k += 1
    if has_res:
        y = refs[k][...] + refs[k + 1][...] * y
        k += 2
    refs[k][...] = y.astype(refs[k].dtype)


def _matmul(a, w, *, name, n_out, tm, tn, out_dtype, layer=None, col_block0=0, wt_row0=None, pre=None, bias=None, res=None,
            gate=None):
    m, kdim = a.shape
    assert m % tm == 0 and (wt_row0 is None or (n_out % tn == 0 and wt_row0 % 8 == 0))
    grid = (m // tm, pl.cdiv(n_out, tn))
    if wt_row0 is not None:
        w_spec = pl.BlockSpec((pl.Element(1), pl.Element(tn), pl.Element(kdim)),
                              lambda i, j: (layer, pl.multiple_of(wt_row0 + j * tn, 8), 0))
    elif layer is None:
        w_spec = pl.BlockSpec((kdim, tn), lambda i, j: (0, j + col_block0))
    else:
        w_spec = pl.BlockSpec((None, kdim, tn), lambda i, j: (layer, 0, j + col_block0))
    in_specs = [pl.BlockSpec((tm, kdim), lambda i, j: (i, 0), pipeline_mode=pl.Buffered(1)), w_spec]
    args = [a, w]
    if bias is not None:
        in_specs.append(pl.BlockSpec((1, tn), lambda i, j: (0, j)))
        args.append(bias)
    if res is not None:
        in_specs += [pl.BlockSpec((tm, tn), lambda i, j: (i, j)), pl.BlockSpec((1, tn), lambda i, j: (0, j))]
        args += [res, gate]
    return pl.pallas_call(
        functools.partial(_mm_kernel, pre=pre, has_bias=bias is not None, has_res=res is not None,
                          w_transposed=wt_row0 is not None),
        grid=grid,
        in_specs=in_specs,
        out_specs=pl.BlockSpec((tm, tn), lambda i, j: (i, j)),
        out_shape=jax.ShapeDtypeStruct((m, n_out), out_dtype),
        compiler_params=_params("parallel", "parallel"),
        name=name,
    )(*args)


def _mm_acc_kernel(a_ref, w_ref, x_ref, g_ref, o_ref, acc_ref):
    @pl.when(pl.program_id(2) == 0)
    def _():
        acc_ref[...] = jnp.zeros_like(acc_ref)

    acc_ref[...] += _dot(a_ref[...], w_ref[...])

    @pl.when(pl.program_id(2) == pl.num_programs(2) - 1)
    def _():
        o_ref[...] = x_ref[...] + g_ref[...] * acc_ref[...]


def _matmul_acc_res(a, w, layer, res, gate, *, tm, tn, tk):
    m, kdim = a.shape
    n = w.shape[2]
    return pl.pallas_call(
        _mm_acc_kernel,
        grid=(m // tm, n // tn, kdim // tk),
        in_specs=[
            pl.BlockSpec((tm, tk), lambda i, j, k: (i, k)),
            pl.BlockSpec((None, tk, tn), lambda i, j, k: (layer, k, j)),
            pl.BlockSpec((tm, tn), lambda i, j, k: (i, j)),
            pl.BlockSpec((1, tn), lambda i, j, k: (0, j)),
        ],
        out_specs=pl.BlockSpec((tm, tn), lambda i, j, k: (i, j)),
        out_shape=jax.ShapeDtypeStruct((m, n), F32),
        scratch_shapes=[pltpu.VMEM((tm, tn), F32)],
        compiler_params=_params("parallel", "parallel", "arbitrary"),
        name="peer_out",
    )(a, w, res, gate)


def _norm_kernel(x_ref, mul_ref, add_ref, o_ref):
    x = x_ref[...]
    ms = jnp.mean(x * x, axis=-1, keepdims=True)
    o_ref[...] = (x * lax.rsqrt(ms + NORM_EPS) * mul_ref[...] + add_ref[...]).astype(o_ref.dtype)


def _rms_mod(x, mul, add, out_dtype, tm=256):
    m, d = x.shape
    return pl.pallas_call(
        _norm_kernel,
        grid=(m // tm,),
        in_specs=[pl.BlockSpec((tm, d), lambda i: (i, 0)), pl.BlockSpec((1, d), lambda i: (0, 0)),
                  pl.BlockSpec((1, d), lambda i: (0, 0))],
        out_specs=pl.BlockSpec((tm, d), lambda i: (i, 0)),
        out_shape=jax.ShapeDtypeStruct((m, d), out_dtype),
        compiler_params=_params("parallel"),
        name="rms_mod",
    )(x, mul, add)


def _hg_lb_kernel(lg_ref, o_ref):
    lg = lg_ref[...]
    e = jnp.exp(lg - jnp.max(lg, axis=0, keepdims=True))
    p = e / jnp.sum(e, axis=0, keepdims=True)
    rows = [jnp.zeros_like(p[0:1])]
    for l in range(1, lg.shape[0]):
        rows.append(rows[-1] + p[l:l + 1])
    o_ref[...] = jnp.concatenate(rows, axis=0)


def _hg_lower_bounds(logits):
    return pl.pallas_call(_hg_lb_kernel, out_shape=jax.ShapeDtypeStruct(logits.shape, F32), name="hg_lb")(logits)


def _ret_kernel(q_ref, k_ref, v_ref, g_ref, cos_ref, sin_ref, dmat_ref, xi_ref, zeta_ref, gam_ref, o_ref, st_ref, *, tb):
    c = RET_CHUNK

    @pl.when(pl.program_id(1) == 0)
    def _():
        st_ref[...] = jnp.zeros_like(st_ref)

    dmat, xi, zeta, gam = dmat_ref[...], xi_ref[...], zeta_ref[...], gam_ref[...]
    for ci in range(tb // c):
        sl = slice(ci * c, (ci + 1) * c)
        q, k, v = q_ref[sl, :], k_ref[sl, :], v_ref[sl, :]
        cos, sin = cos_ref[sl, :], sin_ref[sl, :]
        half = RET_DK // 2
        qr = q * cos + pltpu.roll(q, half, 1) * sin
        kr = (k * cos + pltpu.roll(k, half, 1) * sin) * (RET_DK ** -0.5)
        scores = _dot(qr, kr, _NT) * dmat
        o = _dot(scores, v)
        st = st_ref[...]
        o = o + _dot(qr * xi, st)
        st_ref[...] = st * gam + _dot(kr * zeta, v, _TN)
        ms = jnp.mean(o * o, axis=-1, keepdims=True)
        g = g_ref[sl, :]
        y = o * lax.rsqrt(ms + NORM_EPS) * (g * jax.nn.sigmoid(g))
        o_ref[sl, :] = y.astype(o_ref.dtype)


def _retention(proj, tables, tb=512):
    t = proj.shape[0]
    h = RET_HEADS
    cos, sin, dmat, xi, zeta, gam = tables
    c = RET_CHUNK
    col = lambda g: pl.BlockSpec((tb, 128), lambda hh, n, g=g: (n, g * h + hh))
    tab = pl.BlockSpec((tb, 128), lambda hh, n: (n, 0))
    return pl.pallas_call(
        functools.partial(_ret_kernel, tb=tb),
        grid=(h, t // tb),
        in_specs=[col(0), col(1), col(2), col(3), tab, tab,
                  pl.BlockSpec((None, c, c), lambda hh, n: (hh, 0, 0)),
                  pl.BlockSpec((None, c, 128), lambda hh, n: (hh, 0, 0)),
                  pl.BlockSpec((None, c, 128), lambda hh, n: (hh, 0, 0)),
                  pl.BlockSpec((None, 1, 128), lambda hh, n: (hh, 0, 0))],
        out_specs=pl.BlockSpec((tb, 128), lambda hh, n: (n, hh)),
        out_shape=jax.ShapeDtypeStruct((t, h * 128), BF16),
        scratch_shapes=[pltpu.VMEM((RET_DK, 128), F32)],
        compiler_params=_params("parallel", "arbitrary"),
        name="retention",
    )(proj, proj, proj, proj, cos, sin, dmat, xi, zeta, gam)


def _retention_tables(t):
    half = RET_DK // 2
    inv = ROPE_BASE ** (-jnp.arange(half, dtype=F32) / half)
    ang = jnp.arange(t, dtype=F32)[:, None] * inv[None, :]
    cos, sin = jnp.cos(ang), jnp.sin(ang)
    cos2 = jnp.concatenate([cos, cos], axis=-1)
    sin2 = jnp.concatenate([-sin, sin], axis=-1)
    c = RET_CHUNK
    log_g = jnp.log(1.0 - 2.0 ** (-5.0 - jnp.arange(RET_HEADS, dtype=F32)))
    pos = jnp.arange(c, dtype=F32)
    rel = pos[:, None] - pos[None, :]
    dmat = jnp.where(rel >= 0, jnp.exp(log_g[:, None, None] * jnp.maximum(rel, 0.0)), 0.0)
    xi = jnp.broadcast_to(jnp.exp(log_g[:, None] * (pos + 1.0))[:, :, None], (RET_HEADS, c, 128))
    zeta = jnp.broadcast_to(jnp.exp(log_g[:, None] * (c - 1.0 - pos))[:, :, None], (RET_HEADS, c, 128))
    gam = jnp.broadcast_to(jnp.exp(log_g * c)[:, None, None], (RET_HEADS, 1, 128))
    return cos2, sin2, dmat, xi, zeta, gam


def _hg_kernel(q_ref, f_ref, v_ref, g_ref, lb_ref, o_ref, st_ref, *, tb):
    n = HG_SUB

    @pl.when(pl.program_id(1) == 0)
    def _():
        st_ref[...] = jnp.zeros_like(st_ref)

    nb = tb // n
    lb = lb_ref[...]
    fg = lb + (1.0 - lb) * jax.nn.sigmoid(f_ref[...])
    kk = (1.0 - fg).reshape(nb, n, 128)
    cum = _dot_sel(_block_tril(tb, n), jnp.log(fg)).reshape(nb, n, 128)
    q, v = q_ref[...].reshape(nb, n, 128), v_ref[...].reshape(nb, n, 128)
    row = lax.broadcasted_iota(I32, (nb, n, 128), 1)
    o = jnp.zeros((nb, n, 128), F32)
    for s in range(n):
        d = jnp.where(row >= s, cum - cum[:, s:s + 1, :], -jnp.inf)
        col = jnp.sum(q * jnp.exp(d) * kk[:, s:s + 1, :], axis=-1, keepdims=True)
        o = o + col * v[:, s:s + 1, :]
    last = cum[:, n - 1:n, :]
    upd = lax.dot_general(v.astype(BF16), (kk * jnp.exp(last - cum)).astype(BF16), _BTN, preferred_element_type=F32)
    dec = jnp.exp(last)
    st = st_ref[...]
    entering = []
    for j in range(nb):
        entering.append(st)
        st = st * dec[j] + upd[j]
    st_ref[...] = st
    o = o + lax.dot_general((q * jnp.exp(cum)).astype(BF16), jnp.stack(entering).astype(BF16), _BNT,
                            preferred_element_type=F32)
    o = o.reshape(tb, 128)
    ms = jnp.mean(o * o, axis=-1, keepdims=True)
    g = g_ref[...]
    o_ref[...] = (o * lax.rsqrt(ms + NORM_EPS) * (g * jax.nn.sigmoid(g))).astype(o_ref.dtype)


def _hgrn2(proj, lb, tb=256):
    t = proj.shape[0]
    h = HG_HEADS
    col = lambda g: pl.BlockSpec((tb, 128), lambda hh, n, g=g: (n, (4 + g) * h + hh))
    return pl.pallas_call(
        functools.partial(_hg_kernel, tb=tb),
        grid=(h, t // tb),
        in_specs=[col(0), col(1), col(2), col(3), pl.BlockSpec((1, 128), lambda hh, n: (0, hh))],
        out_specs=pl.BlockSpec((tb, 128), lambda hh, n: (n, hh)),
        out_shape=jax.ShapeDtypeStruct((t, h * 128), BF16),
        scratch_shapes=[pltpu.VMEM((128, HG_DK), F32)],
        compiler_params=_params("parallel", "arbitrary"),
        name="hgrn2",
    )(proj, proj, proj, proj, lb)


def _rw_prep_kernel(*refs, has_vres):
    if has_vres:
        (cur_ref, prv_ref, mu_ref, w0_ref, w2_ref, a0_ref, a2_ref, g2_ref, vf_ref, v0_ref, v1_ref, v2_ref,
         r_ref, ld_ref, k_ref, v_ref, a_ref, g_ref) = refs
    else:
        (cur_ref, prv_ref, mu_ref, w0_ref, w2_ref, a0_ref, a2_ref, g2_ref,
         r_ref, ld_ref, k_ref, v_ref, a_ref, g_ref) = refs
    cur = cur_ref[...]
    tt = cur.shape[0]
    prev_row = jnp.where(pl.program_id(0) == 0, 0.0, prv_ref[7:8, :])
    row = lax.broadcasted_iota(I32, cur.shape, 0)
    shifted = jnp.where(row == 0, prev_row, pltpu.roll(cur, 1, 0))
    cols = cur + (shifted - cur) * mu_ref[...]
    w = RW_W
    r, k, v = cols[:, 0:w], cols[:, w:2 * w], cols[:, 2 * w:3 * w]
    wa = cols[:, 3 * w:3 * w + 128]
    lane = lax.broadcasted_iota(I32, wa.shape, 1)
    wa = jnp.where(lane < 64, jnp.tanh(wa), wa)
    gl = cols[:, 3 * w + 128:3 * w + 256]
    wlog = -jax.nn.softplus(-(w0_ref[...] + _dot(wa, w2_ref[...]))) - 0.5
    a = jax.nn.sigmoid(a0_ref[...] + _dot(wa, a2_ref[...]))
    if has_vres:
        mix = jax.nn.sigmoid(v0_ref[...] + _dot(_dot(v, v1_ref[...]), v2_ref[...]))
        v = v + (vf_ref[...] - v) * mix
    r_ref[...] = r
    ld_ref[...] = -jnp.exp(wlog)
    k_ref[...] = k
    v_ref[...] = v
    a_ref[...] = a
    g_ref[...] = _dot(jax.nn.sigmoid(gl), g2_ref[...])


def _rw_prep(cols, mu, w0, w2p, a0, a2p, g2, vres, tt=256):
    t, nc = cols.shape
    w = RW_W
    full = lambda shape: pl.BlockSpec(shape, lambda i: (0,) * len(shape))
    in_specs = [pl.BlockSpec((tt, nc), lambda i: (i, 0)),
                pl.BlockSpec((8, nc), lambda i: (jnp.maximum(i * (tt // 8) - 1, 0), 0)),
                full((1, nc)), full((1, w)), full((128, w)), full((1, w)), full((128, w)), full((128, w))]
    args = [cols, cols, mu, w0, w2p, a0, a2p, g2]
    if vres is not None:
        v_first, v0, v1p, v2p = vres
        in_specs += [pl.BlockSpec((tt, w), lambda i: (i, 0)), full((1, w)), full((w, 128)), full((128, w))]
        args += [v_first, v0, v1p, v2p]
    out = jax.ShapeDtypeStruct((t, w), F32)
    return pl.pallas_call(
        functools.partial(_rw_prep_kernel, has_vres=vres is not None),
        grid=(t // tt,),
        in_specs=in_specs,
        out_specs=[pl.BlockSpec((tt, w), lambda i: (i, 0))] * 6,
        out_shape=[out] * 6,
        compiler_params=_params("parallel"),
        name="rwkv_prep",
    )(*args)


def _rw_level_masks(c):
    ti = lax.broadcasted_iota(I32, (c, c), 0)
    si = lax.broadcasted_iota(I32, (c, c), 1)
    masks = []
    n = 1
    while n < c:
        same = (ti // (2 * n)) == (si // (2 * n))
        masks.append(same & ((ti // n) % 2 == 1) & ((si // n) % 2 == 0))
        n *= 2
    return ti, si, masks


def _bdot(a, b, dims=_BNN):
    return lax.dot_general(a.astype(BF16), b.astype(BF16), dims, preferred_element_type=F32)


def _rw_kernel(r_ref, ld_ref, k_ref, v_ref, a_ref, g_ref, kk_ref, ka_ref, rk_ref, lnw_ref, lnb_ref, o_ref, st_ref, *, tb):
    c, nh, hd = RW_CHUNK, RW_HEADS, RW_HD
    nc = tb // c

    @pl.when(pl.program_id(0) == 0)
    def _():
        st_ref[...] = jnp.zeros_like(st_ref)

    def split(x):
        return jnp.stack([x[ci * c:(ci + 1) * c, h * hd:(h + 1) * hd] for ci in range(nc) for h in range(nh)])

    def split_row(ref):
        x = ref[...]
        return jnp.stack([x[:, h * hd:(h + 1) * hd] for _ in range(nc) for h in range(nh)])

    ld2 = ld_ref[...]
    cw2 = _dot_sel(_block_tril(tb, c), ld2)
    r, ld, cw, k, v, a = split(r_ref[...]), split(ld2), split(cw2), split(k_ref[...]), split(v_ref[...]), split(a_ref[...])
    kk_w, ka_w, rk_w, lnw, lnb = (split_row(p) for p in (kk_ref, ka_ref, rk_ref, lnw_ref, lnb_ref))
    ti, si, masks = _rw_level_masks(c)
    tril, strict = (ti >= si)[None], (ti > si)[None]

    wc = jnp.exp(cw)
    wprev = jnp.exp(cw - ld)
    winv = jnp.exp(-cw)
    wc_last = wc[:, c - 1:c, :]
    wtail = wc_last * winv
    kkf = k * kk_w
    kk = kkf / jnp.maximum(jnp.sqrt(jnp.sum(kkf * kkf, axis=-1, keepdims=True)), 1e-12)
    k2 = k * (1.0 + (a - 1.0) * ka_w)
    bv = kk * a
    at = -kk * wprev
    rt = r * wc
    ar = jnp.concatenate([at, rt], axis=1)
    gb = _bdot(ar, bv * winv, _BNT)
    gk = _bdot(ar, k2 * winv, _BNT)
    lab = jnp.where(strict, gb[:, :c], 0.0)
    lak = jnp.where(strict, gk[:, :c], 0.0)
    mrb = jnp.where(tril, gb[:, c:], 0.0)
    mrk = jnp.where(tril, gk[:, c:], 0.0)
    x = jnp.where((ti == si)[None], 1.0, 0.0) + jnp.where(masks[0][None], lab, 0.0)
    for m in masks[1:]:
        x = x + _bdot(_bdot(x, jnp.where(m[None], lab, 0.0)), x)
    lv = _bdot(jnp.concatenate([lak, mrk], axis=1), v)
    p = _bdot(x, at)
    q = _bdot(x, lv[:, :c])
    r2 = rt + _bdot(mrb, p)
    y0 = _bdot(mrb, q) + lv[:, c:]
    bh = bv * wtail
    g = _bdot(p, bh, _BTN)
    s_add = _bdot(q, bh, _BTN) + _bdot(v, k2 * wtail, _BTN)
    s = st_ref[...]
    entering = []
    for ci in range(nc):
        b0, b1 = ci * nh, (ci + 1) * nh
        entering.append(s)
        s = s * wc_last[b0:b1] + _bdot(s, g[b0:b1]) + s_add[b0:b1]
    st_ref[...] = s
    y = _bdot(r2, jnp.concatenate(entering, axis=0), _BNT) + y0
    mean = jnp.mean(y, axis=-1, keepdims=True)
    var = jnp.mean(jnp.square(y - mean), axis=-1, keepdims=True)
    yn = (y - mean) * lax.rsqrt(var + RW_LN_EPS) * lnw + lnb
    out = yn + jnp.sum(r * k2 * rk_w, axis=-1, keepdims=True) * v
    for ci in range(nc):
        rows = jnp.concatenate([out[ci * nh + h] for h in range(nh)], axis=1)
        o_ref[ci * c:(ci + 1) * c, :] = (rows * g_ref[ci * c:(ci + 1) * c, :]).astype(o_ref.dtype)


def _rwkv7(prep, k_k, k_a, r_k, ln_w, ln_b, tb=128):
    r, ld, k, v, a, g = prep
    t = r.shape[0]
    col = pl.BlockSpec((tb, RW_W), lambda n: (n, 0))
    par = pl.BlockSpec((1, RW_W), lambda n: (0, 0))
    return pl.pallas_call(
        functools.partial(_rw_kernel, tb=tb),
        grid=(t // tb,),
        in_specs=[col] * 6 + [par] * 5,
        out_specs=col,
        out_shape=jax.ShapeDtypeStruct((t, RW_W), BF16),
        scratch_shapes=[pltpu.VMEM((RW_HEADS, RW_HD, RW_HD), F32)],
        compiler_params=_params("arbitrary"),
        name="rwkv7",
    )(r, ld, k, v, a, g, k_k, k_a, r_k, ln_w, ln_b)


def _fox_cum_kernel(h_ref, wt_ref, w_ref, fbc_ref, fbr_ref, row_ref, col_ref, crow_ref, ccol_ref):
    @pl.when(pl.program_id(0) == 0)
    def _():
        crow_ref[...] = jnp.zeros_like(crow_ref)
        ccol_ref[...] = jnp.zeros_like(ccol_ref)

    h = h_ref[...]
    tt = h.shape[0]
    ti = lax.broadcasted_iota(I32, (tt, tt), 0)
    si = lax.broadcasted_iota(I32, (tt, tt), 1)
    lf_r = jax.nn.log_sigmoid(_dot(wt_ref[...], h, _NT) + fbc_ref[...])
    cum_r = _dot_sel((ti <= si).astype(F32), lf_r, x_first=True) + crow_ref[:, 0:1]
    row_ref[...] = cum_r
    crow_ref[...] = jnp.broadcast_to(cum_r[:, tt - 1:tt], crow_ref.shape)
    lf_c = jax.nn.log_sigmoid(_dot(h, w_ref[...]) + fbr_ref[...])
    cum_c = _dot_sel((ti >= si).astype(F32), lf_c) + ccol_ref[0:1, :]
    col_ref[...] = cum_c
    ccol_ref[...] = jnp.broadcast_to(cum_c[tt - 1:tt, :], ccol_ref.shape)


def _fox_cum(h, w_ff_t, fb, tt=512):
    t, d = h.shape
    nh = FOX_HEADS
    full = lambda shape: pl.BlockSpec(shape, lambda i: (0,) * len(shape))
    return pl.pallas_call(
        _fox_cum_kernel,
        grid=(t // tt,),
        in_specs=[pl.BlockSpec((tt, d), lambda i: (i, 0)), full((nh, d)), full((d, nh)), full((nh, 1)), full((1, nh))],
        out_specs=[pl.BlockSpec((nh, tt), lambda i: (0, i)), pl.BlockSpec((tt, nh), lambda i: (i, 0))],
        out_shape=[jax.ShapeDtypeStruct((nh, t), F32), jax.ShapeDtypeStruct((t, nh), F32)],
        scratch_shapes=[pltpu.VMEM((nh, 128), F32), pltpu.VMEM((8, nh), F32)],
        compiler_params=_params("arbitrary"),
        name="fox_cum",
    )(h, w_ff_t, w_ff_t.T, fb.reshape(nh, 1), fb.reshape(1, nh))


def _fox_kernel(q_ref, k_ref, v_ref, cq_ref, ck_ref, o_ref, m_ref, l_ref, acc_ref, *, tq):
    hh, qi = pl.program_id(0), pl.program_id(1)
    nk = k_ref.shape[0] // tq
    q = q_ref[...]
    lane = lax.broadcasted_iota(I32, cq_ref.shape, 1)
    cq = jnp.sum(jnp.where(lane == hh, cq_ref[...], 0.0), axis=-1, keepdims=True)
    m_ref[...] = jnp.full_like(m_ref, -jnp.inf)
    l_ref[...] = jnp.zeros_like(l_ref)
    acc_ref[...] = jnp.zeros_like(acc_ref)

    def block(ki, diagonal):
        rows = pl.ds(pl.multiple_of(ki * tq, tq), tq)
        s = _dot(q, k_ref[rows, :], _NT) * (FOX_HD ** -0.5)
        s = s + cq - ck_ref[pl.ds(hh * nk + ki, 1), :]
        if diagonal:
            s = jnp.where(lax.broadcasted_iota(I32, s.shape, 1) <= lax.broadcasted_iota(I32, s.shape, 0), s, -jnp.inf)
        m_old = m_ref[...]
        m_new = jnp.maximum(m_old, jnp.max(s, axis=-1, keepdims=True))
        alpha = jnp.exp(m_old - m_new)
        p = jnp.exp(s - m_new)
        l_ref[...] = alpha * l_ref[...] + jnp.sum(p, axis=-1, keepdims=True)
        acc_ref[...] = alpha * acc_ref[...] + _dot(p, v_ref[rows, :])
        m_ref[...] = m_new

    def body(ki, carry):
        block(ki, False)
        return carry

    lax.fori_loop(0, qi, body, 0)
    block(qi, True)
    o_ref[...] = (acc_ref[...] / l_ref[...]).astype(o_ref.dtype)


def _fox_attention(qkv, cum_row, cum_col, tq=2048):
    t = qkv.shape[0]
    h = FOX_HEADS
    nq = t // tq
    return pl.pallas_call(
        functools.partial(_fox_kernel, tq=tq),
        grid=(h, nq),
        in_specs=[pl.BlockSpec((tq, 128), lambda hh, qi: (qi, hh)),
                  pl.BlockSpec((t, 128), lambda hh, qi: (0, h + hh)),
                  pl.BlockSpec((t, 128), lambda hh, qi: (0, 2 * h + hh)),
                  pl.BlockSpec((tq, h), lambda hh, qi: (qi, 0)),
                  pl.BlockSpec((h * nq, tq), lambda hh, qi: (0, 0))],
        out_specs=pl.BlockSpec((tq, 128), lambda hh, qi: (qi, hh)),
        out_shape=jax.ShapeDtypeStruct((t, h * 128), BF16),
        scratch_shapes=[pltpu.VMEM((tq, 1), F32), pltpu.VMEM((tq, 1), F32), pltpu.VMEM((tq, 128), F32)],
        compiler_params=_params("parallel", "arbitrary"),
        name="fox_attention",
    )(qkv, qkv, qkv, cum_col, cum_row.reshape(h * nq, tq))


def _merge_kernel(*refs):
    y_refs, w_refs, g_refs, o_ref = refs[0:4], refs[4:8], refs[8:12], refs[12]
    acc = None
    for n in range(N_BRANCH):
        term = jax.nn.sigmoid(g_refs[n][...]) * _dot(y_refs[n][...], w_refs[n][...])
        acc = term if acc is None else acc + term
    o_ref[...] = acc.astype(o_ref.dtype)


def _merge(branches, w_branch, layer, gates, tm=1024, tn=512):
    t = branches[0].shape[0]
    d = D_MODEL
    nb = d // tn
    y_spec = pl.BlockSpec((tm, BRANCH_W), lambda i, j: (i, 0))
    w_specs = [pl.BlockSpec((None, None, BRANCH_W, tn), lambda i, j, n=n: (layer, n, 0, j)) for n in range(N_BRANCH)]
    g_specs = [pl.BlockSpec((tm, tn), lambda i, j, n=n: (i, n * nb + j)) for n in range(N_BRANCH)]
    return pl.pallas_call(
        _merge_kernel,
        grid=(t // tm, nb),
        in_specs=[y_spec] * 4 + w_specs + g_specs,
        out_specs=pl.BlockSpec((tm, tn), lambda i, j: (i, j)),
        out_shape=jax.ShapeDtypeStruct((t, d), BF16),
        compiler_params=_params("parallel", "parallel"),
        name="merge",
    )(*branches, *([w_branch] * N_BRANCH), *([gates] * N_BRANCH))


def _topk_rows(s, k, payloads=()):
    n = s.shape[0]
    row = lax.broadcasted_iota(I32, s.shape, 0).astype(F32)
    vals, rows, picked = [], [], [[] for _ in payloads]
    for _ in range(k):
        m = jnp.max(s, axis=0, keepdims=True)
        first = jnp.min(jnp.where(s == m, row, float(n)), axis=0, keepdims=True)
        sel = row == first
        vals.append(m)
        rows.append(first)
        for out, p in zip(picked, payloads):
            out.append(jnp.max(jnp.where(sel, p, -1.0), axis=0, keepdims=True))
        s = jnp.where(sel, -jnp.inf, s)
    cat = lambda xs: jnp.concatenate(xs, axis=0)
    return cat(vals), cat(rows), [cat(p) for p in picked]


_PEER_PAIRS = [(r1, r2) for r1 in range(PEER_TOPK) for r2 in range(PEER_TOPK) if (r1 + 1) * (r2 + 1) <= PEER_TOPK]


def _peer_topk_kernel(q_ref, keys_ref, a_ref, b_ref, g_ref):
    kq = PEER_TOPK
    half = PEER_DKEY // 2
    a_rows, b_rows, g_rows = [], [], []
    for hh in range(PEER_HEADS):
        top = []
        for p in range(2):
            qs = q_ref[:, (2 * hh + p) * half:(2 * hh + p + 1) * half]
            s = _dot(keys_ref[hh, p], qs, _NT)
            top.append(_topk_rows(s, kq)[:2])
        (v1, i1), (v2, i2) = top
        cand = jnp.concatenate([v1[r1:r1 + 1, :] + v2[r2:r2 + 1, :] for r1, r2 in _PEER_PAIRS], axis=0)
        key1 = jnp.concatenate([i1[r1:r1 + 1, :] for r1, _ in _PEER_PAIRS], axis=0)
        key2 = jnp.concatenate([i2[r2:r2 + 1, :] for _, r2 in _PEER_PAIRS], axis=0)
        best, _, (e1, e2) = _topk_rows(cand, kq, (key1, key2))
        ex = jnp.exp(best - best[0:1, :])
        a_rows.append(e1)
        b_rows.append(e2)
        g_rows.append(ex / jnp.sum(ex, axis=0, keepdims=True))
    a_ref[...] = jnp.concatenate(a_rows, axis=0).T.astype(I32)
    b_ref[...] = jnp.concatenate(b_rows, axis=0).T.astype(I32)
    g_ref[...] = jnp.concatenate(g_rows, axis=0).T


def _peer_topk(q, keys, tt=256):
    t = q.shape[0]
    n = PEER_HEADS * PEER_TOPK
    spec = pl.BlockSpec((tt, n), lambda i: (i, 0))
    return pl.pallas_call(
        _peer_topk_kernel,
        grid=(t // tt,),
        in_specs=[pl.BlockSpec((tt, q.shape[1]), lambda i: (i, 0)),
                  pl.BlockSpec(keys.shape, lambda i: (0, 0, 0, 0))],
        out_specs=[spec, spec, spec],
        out_shape=[jax.ShapeDtypeStruct((t, n), I32), jax.ShapeDtypeStruct((t, n), I32), jax.ShapeDtypeStruct((t, n), F32)],
        compiler_params=_params("parallel"),
        name="peer_topk",
    )(q, keys)


def _peer_weights_kernel(a_ref, b_ref, g_ref, o_ref):
    tt = a_ref.shape[0]
    nk = PEER_NKEYS
    sub = lax.broadcasted_iota(I32, (nk, a_ref.shape[1]), 0)

    group = 8

    def body(i, carry):
        t0 = pl.multiple_of(i * group, group)
        lefts, rights = [], []
        for u in range(group):
            a = a_ref[pl.ds(t0 + u, 1), :]
            b = b_ref[pl.ds(t0 + u, 1), :]
            g = g_ref[pl.ds(t0 + u, 1), :]
            g_hi = g.astype(BF16).astype(F32)
            g_lo = g - g_hi
            lefts.append(jnp.concatenate([jnp.where(sub == a, g_hi, 0.0), jnp.where(sub == a, g_lo, 0.0)], axis=1))
            onehot = jnp.where(sub == b, 1.0, 0.0)
            rights.append(jnp.concatenate([onehot, onehot], axis=1))
        o_ref[pl.ds(t0, group)] = _bdot(jnp.stack(lefts), jnp.stack(rights), _BNT)
        return carry

    lax.fori_loop(0, tt // group, body, 0)


def _peer_weights(a, b, g, tt=64):
    t, n = a.shape
    spec = pl.BlockSpec((tt, n), lambda i: (i, 0))
    return pl.pallas_call(
        _peer_weights_kernel,
        grid=(t // tt,),
        in_specs=[spec, spec, spec],
        out_specs=pl.BlockSpec((tt, PEER_NKEYS, PEER_NKEYS), lambda i: (i, 0, 0)),
        out_shape=jax.ShapeDtypeStruct((t, PEER_NKEYS, PEER_NKEYS), F32),
        compiler_params=_params("parallel"),
        name="peer_weights",
    )(a, b, g)


def _peer_act_kernel(h_ref, u_ref, w_ref, o_ref, *, ne1):
    h = h_ref[...]
    nk = PEER_NKEYS
    for j in range(0, ne1, 2):
        z = _dot(h, u_ref[j * nk:(j + 2) * nk, :], _NT)
        gelu = 0.5 * z * (1.0 + lax.erf(z * (2.0 ** -0.5)))
        w = jnp.concatenate([w_ref[:, j, :], w_ref[:, j + 1, :]], axis=1)
        o_ref[:, j * nk:(j + 2) * nk] = (gelu * w).astype(o_ref.dtype)


def _peer_act(h, u, layer, wts, tm=1024, te=1024):
    t, d = h.shape
    e = u.shape[1]
    ne1 = te // PEER_NKEYS
    return pl.pallas_call(
        functools.partial(_peer_act_kernel, ne1=ne1),
        grid=(t // tm, e // te),
        in_specs=[pl.BlockSpec((tm, d), lambda i, j: (i, 0), pipeline_mode=pl.Buffered(1)),
                  pl.BlockSpec((None, te, d), lambda i, j: (layer, j, 0)),
                  pl.BlockSpec((tm, ne1, PEER_NKEYS), lambda i, j: (i, j, 0))],
        out_specs=pl.BlockSpec((tm, te), lambda i, j: (i, j)),
        out_shape=jax.ShapeDtypeStruct((t, e), BF16),
        compiler_params=_params("parallel", "parallel"),
        name="peer_act",
    )(h, u, wts)


def kernel(x, c, w_in, w_branch, w_out, ada_w, ada_b, ada_table, hg_lb_logits, rw_mu, rw_w0, rw_w2, rw_a0, rw_a2, rw_g2, rw_k_k, rw_k_a, rw_r_k, rw_ln_w, rw_ln_b, rw_v0, rw_v1, rw_v2, fox_fb, peer_wq, peer_keys, peer_u, peer_v, final_norm_w):
    b, t, d = x.shape
    assert b == 1 and d == D_MODEL
    assert w_in.shape[2] == COL_GATE + N_BRANCH * d and COL_RW % 256 == 0 and COL_FOX % 256 == 0 and COL_FF % 256 == 0
    depth = w_in.shape[0]
    x = x.reshape(t, d)

    c8 = jnp.broadcast_to(c, (8, d))
    mods = _matmul(c8, ada_w, name="ada_mods", n_out=N_MOD * d, tm=8, tn=512, out_dtype=F32, pre="silu",
                   bias=ada_b.reshape(1, -1))[0].reshape(N_MOD, d)
    lbs = _hg_lower_bounds(hg_lb_logits)
    ret_tables = _retention_tables(t)
    w_in_t = jnp.swapaxes(w_in, 1, 2)
    w_ff_t = w_in_t[:, COL_FF:COL_GATE, :]
    u_bf, v_bf = peer_u.astype(BF16), peer_v.astype(BF16)
    zpad = lambda z, axis, n: jnp.pad(z, [(0, n - z.shape[i]) if i == axis else (0, 0) for i in range(z.ndim)])

    v_first = None
    for l in range(depth):
        m = mods + ada_table[l]
        sh1, sc1, g1, sh2, sc2, g2 = [m[i:i + 1] for i in range(N_MOD)]
        h = _rms_mod(x, 1.0 + sc1, sh1, BF16)
        p_rh = _matmul(h, w_in_t, name="proj_ret_hg", layer=l, wt_row0=0, n_out=COL_RW, tm=MM_TM, tn=MM_TN, out_dtype=F32)
        p_rw = _matmul(h, w_in_t, name="proj_rwkv", layer=l, wt_row0=COL_RW, n_out=RW_COLS, tm=MM_TM, tn=MM_TN // 2,
                       out_dtype=F32)
        p_fox = _matmul(h, w_in_t, name="proj_fox", layer=l, wt_row0=COL_FOX, n_out=COL_FF - COL_FOX, tm=MM_TM, tn=MM_TN,
                        out_dtype=BF16)
        p_gate = _matmul(h, w_in_t, name="proj_gate", layer=l, wt_row0=COL_GATE, n_out=N_BRANCH * d, tm=MM_TM, tn=MM_TN,
                         out_dtype=F32)
        ya = _retention(p_rh, ret_tables)
        yb = _hgrn2(p_rh, lbs[l:l + 1])
        w2p = jnp.concatenate([rw_w2[l], jnp.zeros_like(rw_a2[l])], axis=0)
        a2p = jnp.concatenate([jnp.zeros_like(rw_w2[l]), rw_a2[l]], axis=0)
        vres = None
        if l > 0:
            vres = (v_first, rw_v0[l - 1][None], zpad(rw_v1[l - 1], 1, 128), zpad(rw_v2[l - 1], 0, 128))
        prep = _rw_prep(p_rw, rw_mu[l][None], rw_w0[l][None], w2p, rw_a0[l][None], a2p, rw_g2[l], vres)
        if l == 0:
            v_first = prep[3]
        yc = _rwkv7(prep, rw_k_k[l][None], rw_k_a[l][None], rw_r_k[l].reshape(1, RW_W), rw_ln_w[l][None], rw_ln_b[l][None])
        cum_row, cum_col = _fox_cum(h, w_ff_t[l], fox_fb[l])
        yd = _fox_attention(p_fox, cum_row, cum_col)
        merged = _merge([ya, yb, yc, yd], w_branch, l, p_gate)
        x = _matmul(merged, w_out, name="out_proj", layer=l, n_out=d, tm=MM_TM, tn=MM_TN, out_dtype=F32, res=x, gate=g1)
        h2 = _rms_mod(x, 1.0 + sc2, sh2, BF16)
        q = _matmul(h2, peer_wq, name="peer_query", layer=l, n_out=PEER_HEADS * PEER_DKEY, tm=MM_TM, tn=MM_TN, out_dtype=F32)
        ea, eb, eg = _peer_topk(q, peer_keys[l])
        wts = _peer_weights(ea, eb, eg)
        act = _peer_act(h2, u_bf, l, wts)
        x = _matmul_acc_res(act, v_bf, l, x, g2, tm=1024, tn=1024, tk=2048)
    out = _rms_mod(x, final_norm_w[None], jnp.zeros((1, d), F32), F32)
    return out.reshape(b, t, d)
```

```python
import functools

import jax
import jax.numpy as jnp
import numpy as np
from jax import lax
from jax.experimental import pallas as pl
from jax.experimental.pallas import tpu as pltpu

F32, BF16, I32 = jnp.float32, jnp.bfloat16, jnp.int32

D_MODEL = 4096
NORM_EPS = 1e-6
N_MOD = 6
RET_HEADS, RET_DK, RET_CHUNK, ROPE_BASE = 8, 128, 128, 10000.0
HG_HEADS, HG_DK = 8, 128
HG_SUB = 16
RW_HEADS, RW_HD = 16, 64
RW_W = RW_HEADS * RW_HD
RW_CHUNK = 64
RW_LN_EPS = 64e-5
FOX_HEADS, FOX_HD = 8, 128
N_BRANCH, BRANCH_W = 4, 1024
PEER_HEADS, PEER_NKEYS, PEER_DKEY, PEER_TOPK = 8, 128, 256, 16
PEER_EXPERTS = PEER_NKEYS * PEER_NKEYS
COL_RW = 8192
RW_COLS = 3 * RW_W + 64 + 64 + 128
COL_FOX = COL_RW + RW_COLS
COL_FF = COL_FOX + 3 * FOX_HEADS * FOX_HD
COL_GATE = COL_FF + FOX_HEADS

V7X_VMEM_LIMIT_BYTES = 56 * 1024 * 1024
MM_TM, MM_TN = 2048, 512

_NN = (((1,), (0,)), ((), ()))
_NT = (((1,), (1,)), ((), ()))
_TN = (((0,), (0,)), ((), ()))


def _dot(a, b, dims=_NN):
    return lax.dot_general(a.astype(BF16), b.astype(BF16), dims, preferred_element_type=F32)


_BNN = (((2,), (1,)), ((0,), (0,)))
_BNT = (((2,), (2,)), ((0,), (0,)))
_BTN = (((1,), (1,)), ((0,), (0,)))


def _block_tril(size, block):
    ti = lax.broadcasted_iota(I32, (size, size), 0)
    si = lax.broadcasted_iota(I32, (size, size), 1)
    return ((ti >= si) & (ti // block == si // block)).astype(F32)


def _dot_sel(sel, x, x_first=False):
    hi = x.astype(BF16)
    r1 = x - hi.astype(F32)
    mid = r1.astype(BF16)
    lo = (r1 - mid.astype(F32)).astype(BF16)
    s = sel.astype(BF16)
    out = None
    for part in (hi, mid, lo):
        ab = (part, s) if x_first else (s, part)
        term = lax.dot_general(*ab, _NN, preferred_element_type=F32)
        out = term if out is None else out + term
    return out


def _params(*sem):
    return pltpu.CompilerParams(dimension_semantics=sem, vmem_limit_bytes=V7X_VMEM_LIMIT_BYTES)


def _mm_kernel(*refs, pre, has_bias, has_res, w_transposed):
    a_ref, w_ref = refs[0], refs[1]
    k = 2
    a = a_ref[...]
    if pre == "silu":
        a = a * jax.nn.sigmoid(a)
    y = _dot(a, w_ref[0], _NT) if w_transposed else _dot(a, w_ref[...])
    if has_bias:
        y = y + refs[k][...]
        k += 1
    if has_res:
        y = refs[k][...] + refs[k + 1][...] * y
        k += 2
    refs[k][...] = y.astype(refs[k].dtype)


def _matmul(a, w, *, name, n_out, tm, tn, out_dtype, layer=None, col_block0=0, wt_row0=None, pre=None, bias=None, res=None,
            gate=None):
    m, kdim = a.shape
    assert m % tm == 0 and (wt_row0 is None or (n_out % tn == 0 and wt_row0 % 8 == 0))
    grid = (m // tm, pl.cdiv(n_out, tn))
    if wt_row0 is not None:
        w_spec = pl.BlockSpec((pl.Element(1), pl.Element(tn), pl.Element(kdim)),
                              lambda i, j: (layer, pl.multiple_of(wt_row0 + j * tn, 8), 0))
    elif layer is None:
        w_spec = pl.BlockSpec((kdim, tn), lambda i, j: (0, j + col_block0))
    else:
        w_spec = pl.BlockSpec((None, kdim, tn), lambda i, j: (layer, 0, j + col_block0))
    a_mode = pl.Buffered(1) if grid[1] >= 8 else None
    in_specs = [pl.BlockSpec((tm, kdim), lambda i, j: (i, 0), pipeline_mode=a_mode), w_spec]
    args = [a, w]
    if bias is not None:
        in_specs.append(pl.BlockSpec((1, tn), lambda i, j: (0, j)))
        args.append(bias)
    if res is not None:
        in_specs += [pl.BlockSpec((tm, tn), lambda i, j: (i, j)), pl.BlockSpec((1, tn), lambda i, j: (0, j))]
        args += [res, gate]
    return pl.pallas_call(
        functools.partial(_mm_kernel, pre=pre, has_bias=bias is not None, has_res=res is not None,
                          w_transposed=wt_row0 is not None),
        grid=grid,
        in_specs=in_specs,
        out_specs=pl.BlockSpec((tm, tn), lambda i, j: (i, j)),
        out_shape=jax.ShapeDtypeStruct((m, n_out), out_dtype),
        compiler_params=_params("parallel", "parallel"),
        name=name,
    )(*args)


def _mm_acc_kernel(a_ref, w_ref, x_ref, g_ref, o_ref, acc_ref):
    @pl.when(pl.program_id(2) == 0)
    def _():
        acc_ref[...] = jnp.zeros_like(acc_ref)

    acc_ref[...] += _dot(a_ref[...], w_ref[...])

    @pl.when(pl.program_id(2) == pl.num_programs(2) - 1)
    def _():
        o_ref[...] = x_ref[...] + g_ref[...] * acc_ref[...]


def _matmul_acc_res(a, w, layer, res, gate, *, tm, tn, tk):
    m, kdim = a.shape
    n = w.shape[2]
    return pl.pallas_call(
        _mm_acc_kernel,
        grid=(m // tm, n // tn, kdim // tk),
        in_specs=[
            pl.BlockSpec((tm, tk), lambda i, j, k: (i, k)),
            pl.BlockSpec((None, tk, tn), lambda i, j, k: (layer, k, j)),
            pl.BlockSpec((tm, tn), lambda i, j, k: (i, j)),
            pl.BlockSpec((1, tn), lambda i, j, k: (0, j)),
        ],
        out_specs=pl.BlockSpec((tm, tn), lambda i, j, k: (i, j)),
        out_shape=jax.ShapeDtypeStruct((m, n), F32),
        scratch_shapes=[pltpu.VMEM((tm, tn), F32)],
        compiler_params=_params("parallel", "parallel", "arbitrary"),
        name="peer_out",
    )(a, w, res, gate)


def _norm_kernel(x_ref, mul_ref, add_ref, o_ref):
    x = x_ref[...]
    ms = jnp.mean(x * x, axis=-1, keepdims=True)
    o_ref[...] = (x * lax.rsqrt(ms + NORM_EPS) * mul_ref[...] + add_ref[...]).astype(o_ref.dtype)


def _rms_mod(x, mul, add, out_dtype, tm=256):
    m, d = x.shape
    return pl.pallas_call(
        _norm_kernel,
        grid=(m // tm,),
        in_specs=[pl.BlockSpec((tm, d), lambda i: (i, 0)), pl.BlockSpec((1, d), lambda i: (0, 0)),
                  pl.BlockSpec((1, d), lambda i: (0, 0))],
        out_specs=pl.BlockSpec((tm, d), lambda i: (i, 0)),
        out_shape=jax.ShapeDtypeStruct((m, d), out_dtype),
        compiler_params=_params("parallel"),
        name="rms_mod",
    )(x, mul, add)


def _hg_lb_kernel(lg_ref, o_ref):
    lg = lg_ref[...]
    e = jnp.exp(lg - jnp.max(lg, axis=0, keepdims=True))
    p = e / jnp.sum(e, axis=0, keepdims=True)
    rows = [jnp.zeros_like(p[0:1])]
    for l in range(1, lg.shape[0]):
        rows.append(rows[-1] + p[l:l + 1])
    o_ref[...] = jnp.concatenate(rows, axis=0)


def _hg_lower_bounds(logits):
    return pl.pallas_call(_hg_lb_kernel, out_shape=jax.ShapeDtypeStruct(logits.shape, F32), name="hg_lb")(logits)


def _ret_body(q_ref, k_ref, v_ref, g_ref, cos_ref, sin_ref, dmat_ref, xi_ref, zeta_ref, gam_ref, o_ref, st_ref, first):
    c = RET_CHUNK
    dmat, xi, zeta, gam = dmat_ref[...], xi_ref[...], zeta_ref[...], gam_ref[...]
    st = jnp.where(first, 0.0, st_ref[...])
    for ci in range(q_ref.shape[0] // c):
        sl = slice(ci * c, (ci + 1) * c)
        q, k, v = q_ref[sl, :], k_ref[sl, :], v_ref[sl, :]
        cos, sin = cos_ref[sl, :], sin_ref[sl, :]
        half = RET_DK // 2
        qr = q * cos + pltpu.roll(q, half, 1) * sin
        kr = (k * cos + pltpu.roll(k, half, 1) * sin) * (RET_DK ** -0.5)
        scores = _dot(qr, kr, _NT) * dmat
        o = _dot(scores, v) + _dot(qr * xi, st)
        st = st * gam + _dot(kr * zeta, v, _TN)
        ms = jnp.mean(o * o, axis=-1, keepdims=True)
        g = g_ref[sl, :]
        y = o * lax.rsqrt(ms + NORM_EPS) * (g * jax.nn.sigmoid(g))
        o_ref[sl, :] = y.astype(o_ref.dtype)
    st_ref[...] = st


def _ret_kernel(*refs):
    _ret_body(*refs, pl.program_id(1) == 0)


def _retention(proj, tables, tb=512):
    t = proj.shape[0]
    h = RET_HEADS
    cos, sin, dmat, xi, zeta, gam = tables
    c = RET_CHUNK
    col = lambda g: pl.BlockSpec((tb, 128), lambda hh, n, g=g: (n, g * h + hh))
    tab = pl.BlockSpec((tb, 128), lambda hh, n: (n, 0))
    return pl.pallas_call(
        _ret_kernel,
        grid=(h, t // tb),
        in_specs=[col(0), col(1), col(2), col(3), tab, tab,
                  pl.BlockSpec((None, c, c), lambda hh, n: (hh, 0, 0)),
                  pl.BlockSpec((None, c, 128), lambda hh, n: (hh, 0, 0)),
                  pl.BlockSpec((None, c, 128), lambda hh, n: (hh, 0, 0)),
                  pl.BlockSpec((None, 1, 128), lambda hh, n: (hh, 0, 0))],
        out_specs=pl.BlockSpec((tb, 128), lambda hh, n: (n, hh)),
        out_shape=jax.ShapeDtypeStruct((t, h * 128), BF16),
        scratch_shapes=[pltpu.VMEM((RET_DK, 128), F32)],
        compiler_params=_params("parallel", "arbitrary"),
        name="retention",
    )(proj, proj, proj, proj, cos, sin, dmat, xi, zeta, gam)


def _retention_tables(t):
    half = RET_DK // 2
    inv = ROPE_BASE ** (-jnp.arange(half, dtype=F32) / half)
    ang = jnp.arange(t, dtype=F32)[:, None] * inv[None, :]
    cos, sin = jnp.cos(ang), jnp.sin(ang)
    cos2 = jnp.concatenate([cos, cos], axis=-1)
    sin2 = jnp.concatenate([-sin, sin], axis=-1)
    c = RET_CHUNK
    log_g = jnp.log(1.0 - 2.0 ** (-5.0 - jnp.arange(RET_HEADS, dtype=F32)))
    pos = jnp.arange(c, dtype=F32)
    rel = pos[:, None] - pos[None, :]
    dmat = jnp.where(rel >= 0, jnp.exp(log_g[:, None, None] * jnp.maximum(rel, 0.0)), 0.0)
    xi = jnp.broadcast_to(jnp.exp(log_g[:, None] * (pos + 1.0))[:, :, None], (RET_HEADS, c, 128))
    zeta = jnp.broadcast_to(jnp.exp(log_g[:, None] * (c - 1.0 - pos))[:, :, None], (RET_HEADS, c, 128))
    gam = jnp.broadcast_to(jnp.exp(log_g * c)[:, None, None], (RET_HEADS, 1, 128))
    return cos2, sin2, dmat, xi, zeta, gam


def _hg_body(q_ref, f_ref, v_ref, g_ref, lb_ref, o_ref, st_ref, first):
    n = HG_SUB
    tb = q_ref.shape[0]
    nb = tb // n
    lb = lb_ref[...]
    fg = lb + (1.0 - lb) * jax.nn.sigmoid(f_ref[...])
    kk = (1.0 - fg).reshape(nb, n, 128)
    cum = _dot_sel(_block_tril(tb, n), jnp.log(fg)).reshape(nb, n, 128)
    q, v = q_ref[...].reshape(nb, n, 128), v_ref[...].reshape(nb, n, 128)
    row = lax.broadcasted_iota(I32, (nb, n, 128), 1)
    o = jnp.zeros((nb, n, 128), F32)
    for s in range(n):
        d = jnp.where(row >= s, cum - cum[:, s:s + 1, :], -jnp.inf)
        col = jnp.sum(q * jnp.exp(d) * kk[:, s:s + 1, :], axis=-1, keepdims=True)
        o = o + col * v[:, s:s + 1, :]
    last = cum[:, n - 1:n, :]
    upd = lax.dot_general(v.astype(BF16), (kk * jnp.exp(last - cum)).astype(BF16), _BTN, preferred_element_type=F32)
    dec = jnp.exp(last)
    st = jnp.where(first, 0.0, st_ref[...])
    entering = []
    for j in range(nb):
        entering.append(st)
        st = st * dec[j] + upd[j]
    st_ref[...] = st
    o = o + lax.dot_general((q * jnp.exp(cum)).astype(BF16), jnp.stack(entering).astype(BF16), _BNT,
                            preferred_element_type=F32)
    o = o.reshape(tb, 128)
    ms = jnp.mean(o * o, axis=-1, keepdims=True)
    g = g_ref[...]
    o_ref[...] = (o * lax.rsqrt(ms + NORM_EPS) * (g * jax.nn.sigmoid(g))).astype(o_ref.dtype)


def _hg_kernel(*refs):
    _hg_body(*refs, pl.program_id(1) == 0)


def _hgrn2(proj, lb, tb=256):
    t = proj.shape[0]
    h = HG_HEADS
    col = lambda g: pl.BlockSpec((tb, 128), lambda hh, n, g=g: (n, (4 + g) * h + hh))
    return pl.pallas_call(
        _hg_kernel,
        grid=(h, t // tb),
        in_specs=[col(0), col(1), col(2), col(3), pl.BlockSpec((1, 128), lambda hh, n: (0, hh))],
        out_specs=pl.BlockSpec((tb, 128), lambda hh, n: (n, hh)),
        out_shape=jax.ShapeDtypeStruct((t, h * 128), BF16),
        scratch_shapes=[pltpu.VMEM((128, HG_DK), F32)],
        compiler_params=_params("parallel", "arbitrary"),
        name="hgrn2",
    )(proj, proj, proj, proj, lb)


def _rw_prep_kernel(*refs, has_vres):
    if has_vres:
        (cur_ref, prv_ref, mu_ref, w0_ref, w2_ref, a0_ref, a2_ref, g2_ref, vf_ref, v0_ref, v1_ref, v2_ref,
         r_ref, ld_ref, k_ref, v_ref, a_ref, g_ref) = refs
    else:
        (cur_ref, prv_ref, mu_ref, w0_ref, w2_ref, a0_ref, a2_ref, g2_ref,
         r_ref, ld_ref, k_ref, v_ref, a_ref, g_ref) = refs
    cur = cur_ref[...]
    tt = cur.shape[0]
    prev_row = jnp.where(pl.program_id(0) == 0, 0.0, prv_ref[7:8, :])
    row = lax.broadcasted_iota(I32, cur.shape, 0)
    shifted = jnp.where(row == 0, prev_row, pltpu.roll(cur, 1, 0))
    cols = cur + (shifted - cur) * mu_ref[...]
    w = RW_W
    r, k, v = cols[:, 0:w], cols[:, w:2 * w], cols[:, 2 * w:3 * w]
    wa = cols[:, 3 * w:3 * w + 128]
    lane = lax.broadcasted_iota(I32, wa.shape, 1)
    wa = jnp.where(lane < 64, jnp.tanh(wa), wa)
    gl = cols[:, 3 * w + 128:3 * w + 256]
    wlog = -jax.nn.softplus(-(w0_ref[...] + _dot(wa, w2_ref[...]))) - 0.5
    a = jax.nn.sigmoid(a0_ref[...] + _dot(wa, a2_ref[...]))
    if has_vres:
        mix = jax.nn.sigmoid(v0_ref[...] + _dot(_dot(v, v1_ref[...]), v2_ref[...]))
        v = v + (vf_ref[...] - v) * mix
    r_ref[...] = r
    ld_ref[...] = -jnp.exp(wlog)
    k_ref[...] = k
    v_ref[...] = v
    a_ref[...] = a
    g_ref[...] = _dot(jax.nn.sigmoid(gl), g2_ref[...])


def _rw_prep(cols, mu, w0, w2p, a0, a2p, g2, vres, tt=256):
    t, nc = cols.shape
    w = RW_W
    full = lambda shape: pl.BlockSpec(shape, lambda i: (0,) * len(shape))
    in_specs = [pl.BlockSpec((tt, nc), lambda i: (i, 0)),
                pl.BlockSpec((8, nc), lambda i: (jnp.maximum(i * (tt // 8) - 1, 0), 0)),
                full((1, nc)), full((1, w)), full((128, w)), full((1, w)), full((128, w)), full((128, w))]
    args = [cols, cols, mu, w0, w2p, a0, a2p, g2]
    if vres is not None:
        v_first, v0, v1p, v2p = vres
        in_specs += [pl.BlockSpec((tt, w), lambda i: (i, 0)), full((1, w)), full((w, 128)), full((128, w))]
        args += [v_first, v0, v1p, v2p]
    out = jax.ShapeDtypeStruct((t, w), F32)
    return pl.pallas_call(
        functools.partial(_rw_prep_kernel, has_vres=vres is not None),
        grid=(t // tt,),
        in_specs=in_specs,
        out_specs=[pl.BlockSpec((tt, w), lambda i: (i, 0))] * 6,
        out_shape=[out] * 6,
        compiler_params=_params("parallel"),
        name="rwkv_prep",
    )(*args)


def _rw_level_masks(c):
    ti = lax.broadcasted_iota(I32, (c, c), 0)
    si = lax.broadcasted_iota(I32, (c, c), 1)
    masks = []
    n = 1
    while n < c:
        same = (ti // (2 * n)) == (si // (2 * n))
        masks.append(same & ((ti // n) % 2 == 1) & ((si // n) % 2 == 0))
        n *= 2
    return ti, si, masks


def _bdot(a, b, dims=_BNN):
    return lax.dot_general(a.astype(BF16), b.astype(BF16), dims, preferred_element_type=F32)


def _rw_kernel(r_ref, ld_ref, k_ref, v_ref, a_ref, g_ref, kk_ref, ka_ref, rk_ref, lnw_ref, lnb_ref, o_ref, st_ref, *, tb):
    c, nh, hd = RW_CHUNK, RW_HEADS, RW_HD
    nc = tb // c

    @pl.when(pl.program_id(0) == 0)
    def _():
        st_ref[...] = jnp.zeros_like(st_ref)

    def split(x):
        return jnp.stack([x[ci * c:(ci + 1) * c, h * hd:(h + 1) * hd] for ci in range(nc) for h in range(nh)])

    def split_row(ref):
        x = ref[...]
        return jnp.stack([x[:, h * hd:(h + 1) * hd] for _ in range(nc) for h in range(nh)])

    ld2 = ld_ref[...]
    cw2 = _dot_sel(_block_tril(tb, c), ld2)
    r, ld, cw, k, v, a = split(r_ref[...]), split(ld2), split(cw2), split(k_ref[...]), split(v_ref[...]), split(a_ref[...])
    kk_w, ka_w, rk_w, lnw, lnb = (split_row(p) for p in (kk_ref, ka_ref, rk_ref, lnw_ref, lnb_ref))
    ti, si, masks = _rw_level_masks(c)
    tril, strict = (ti >= si)[None], (ti > si)[None]

    wc = jnp.exp(cw)
    wprev = jnp.exp(cw - ld)
    winv = jnp.exp(-cw)
    wc_last = wc[:, c - 1:c, :]
    wtail = wc_last * winv
    kkf = k * kk_w
    kk = kkf / jnp.maximum(jnp.sqrt(jnp.sum(kkf * kkf, axis=-1, keepdims=True)), 1e-12)
    k2 = k * (1.0 + (a - 1.0) * ka_w)
    bv = kk * a
    at = -kk * wprev
    rt = r * wc
    ar = jnp.concatenate([at, rt], axis=1)
    gb = _bdot(ar, bv * winv, _BNT)
    gk = _bdot(ar, k2 * winv, _BNT)
    lab = jnp.where(strict, gb[:, :c], 0.0)
    lak = jnp.where(strict, gk[:, :c], 0.0)
    mrb = jnp.where(tril, gb[:, c:], 0.0)
    mrk = jnp.where(tril, gk[:, c:], 0.0)
    x = jnp.where((ti == si)[None], 1.0, 0.0) + jnp.where(masks[0][None], lab, 0.0)
    for m in masks[1:]:
        x = x + _bdot(_bdot(x, jnp.where(m[None], lab, 0.0)), x)
    lv = _bdot(jnp.concatenate([lak, mrk], axis=1), v)
    p = _bdot(x, at)
    q = _bdot(x, lv[:, :c])
    r2 = rt + _bdot(mrb, p)
    y0 = _bdot(mrb, q) + lv[:, c:]
    bh = bv * wtail
    g = _bdot(p, bh, _BTN)
    s_add = _bdot(q, bh, _BTN) + _bdot(v, k2 * wtail, _BTN)
    s = st_ref[...]
    entering = []
    for ci in range(nc):
        b0, b1 = ci * nh, (ci + 1) * nh
        entering.append(s)
        s = s * wc_last[b0:b1] + _bdot(s, g[b0:b1]) + s_add[b0:b1]
    st_ref[...] = s
    y = _bdot(r2, jnp.concatenate(entering, axis=0), _BNT) + y0
    mean = jnp.mean(y, axis=-1, keepdims=True)
    var = jnp.mean(jnp.square(y - mean), axis=-1, keepdims=True)
    yn = (y - mean) * lax.rsqrt(var + RW_LN_EPS) * lnw + lnb
    out = yn + jnp.sum(r * k2 * rk_w, axis=-1, keepdims=True) * v
    for ci in range(nc):
        rows = jnp.concatenate([out[ci * nh + h] for h in range(nh)], axis=1)
        o_ref[ci * c:(ci + 1) * c, :] = (rows * g_ref[ci * c:(ci + 1) * c, :]).astype(o_ref.dtype)


def _rwkv7(prep, k_k, k_a, r_k, ln_w, ln_b, tb=256):
    r, ld, k, v, a, g = prep
    t = r.shape[0]
    col = pl.BlockSpec((tb, RW_W), lambda n: (n, 0))
    par = pl.BlockSpec((1, RW_W), lambda n: (0, 0))
    return pl.pallas_call(
        functools.partial(_rw_kernel, tb=tb),
        grid=(t // tb,),
        in_specs=[col] * 6 + [par] * 5,
        out_specs=col,
        out_shape=jax.ShapeDtypeStruct((t, RW_W), BF16),
        scratch_shapes=[pltpu.VMEM((RW_HEADS, RW_HD, RW_HD), F32)],
        compiler_params=_params("arbitrary"),
        name="rwkv7",
    )(r, ld, k, v, a, g, k_k, k_a, r_k, ln_w, ln_b)


def _fox_cum_kernel(h_ref, wt_ref, w_ref, fbc_ref, fbr_ref, row_ref, col_ref, crow_ref, ccol_ref):
    @pl.when(pl.program_id(0) == 0)
    def _():
        crow_ref[...] = jnp.zeros_like(crow_ref)
        ccol_ref[...] = jnp.zeros_like(ccol_ref)

    h = h_ref[...]
    tt = h.shape[0]
    ti = lax.broadcasted_iota(I32, (tt, tt), 0)
    si = lax.broadcasted_iota(I32, (tt, tt), 1)
    lf_r = jax.nn.log_sigmoid(_dot(wt_ref[...], h, _NT) + fbc_ref[...])
    cum_r = _dot_sel((ti <= si).astype(F32), lf_r, x_first=True) + crow_ref[:, 0:1]
    row_ref[...] = cum_r
    crow_ref[...] = jnp.broadcast_to(cum_r[:, tt - 1:tt], crow_ref.shape)
    lf_c = jax.nn.log_sigmoid(_dot(h, w_ref[...]) + fbr_ref[...])
    cum_c = _dot_sel((ti >= si).astype(F32), lf_c) + ccol_ref[0:1, :]
    col_ref[...] = cum_c
    ccol_ref[...] = jnp.broadcast_to(cum_c[tt - 1:tt, :], ccol_ref.shape)


def _fox_cum(h, w_ff_t, fb, tt=512):
    t, d = h.shape
    nh = FOX_HEADS
    full = lambda shape: pl.BlockSpec(shape, lambda i: (0,) * len(shape))
    return pl.pallas_call(
        _fox_cum_kernel,
        grid=(t // tt,),
        in_specs=[pl.BlockSpec((tt, d), lambda i: (i, 0)), full((nh, d)), full((d, nh)), full((nh, 1)), full((1, nh))],
        out_specs=[pl.BlockSpec((nh, tt), lambda i: (0, i)), pl.BlockSpec((tt, nh), lambda i: (i, 0))],
        out_shape=[jax.ShapeDtypeStruct((nh, t), F32), jax.ShapeDtypeStruct((t, nh), F32)],
        scratch_shapes=[pltpu.VMEM((nh, 128), F32), pltpu.VMEM((8, nh), F32)],
        compiler_params=_params("arbitrary"),
        name="fox_cum",
    )(h, w_ff_t, w_ff_t.T, fb.reshape(nh, 1), fb.reshape(1, nh))


def _fox_kernel(q_ref, k_ref, v_ref, cq_ref, ck_ref, o_ref, m_ref, l_ref, acc_ref, *, tq):
    hh, qi = pl.program_id(0), pl.program_id(1)
    nk = k_ref.shape[0] // tq
    q = q_ref[...]
    lane = lax.broadcasted_iota(I32, cq_ref.shape, 1)
    cq = jnp.sum(jnp.where(lane == hh, cq_ref[...], 0.0), axis=-1, keepdims=True)
    m_ref[...] = jnp.full_like(m_ref, -jnp.inf)
    l_ref[...] = jnp.zeros_like(l_ref)
    acc_ref[...] = jnp.zeros_like(acc_ref)

    def block(ki, diagonal):
        rows = pl.ds(pl.multiple_of(ki * tq, tq), tq)
        s = _dot(q, k_ref[rows, :], _NT) * (FOX_HD ** -0.5)
        s = s + cq - ck_ref[pl.ds(hh * nk + ki, 1), :]
        if diagonal:
            s = jnp.where(lax.broadcasted_iota(I32, s.shape, 1) <= lax.broadcasted_iota(I32, s.shape, 0), s, -jnp.inf)
        m_old = m_ref[...]
        m_new = jnp.maximum(m_old, jnp.max(s, axis=-1, keepdims=True))
        alpha = jnp.exp(m_old - m_new)
        p = jnp.exp(s - m_new)
        l_ref[...] = alpha * l_ref[...] + jnp.sum(p, axis=-1, keepdims=True)
        acc_ref[...] = alpha * acc_ref[...] + _dot(p, v_ref[rows, :])
        m_ref[...] = m_new

    def body(ki, carry):
        block(ki, False)
        return carry

    lax.fori_loop(0, qi, body, 0)
    block(qi, True)
    o_ref[...] = (acc_ref[...] / l_ref[...]).astype(o_ref.dtype)


def _fox_attention(qkv, cum_row, cum_col, tq=2048):
    t = qkv.shape[0]
    h = FOX_HEADS
    nq = t // tq
    return pl.pallas_call(
        functools.partial(_fox_kernel, tq=tq),
        grid=(h, nq),
        in_specs=[pl.BlockSpec((tq, 128), lambda hh, qi: (qi, hh)),
                  pl.BlockSpec((t, 128), lambda hh, qi: (0, h + hh)),
                  pl.BlockSpec((t, 128), lambda hh, qi: (0, 2 * h + hh)),
                  pl.BlockSpec((tq, h), lambda hh, qi: (qi, 0)),
                  pl.BlockSpec((h * nq, tq), lambda hh, qi: (0, 0))],
        out_specs=pl.BlockSpec((tq, 128), lambda hh, qi: (qi, hh)),
        out_shape=jax.ShapeDtypeStruct((t, h * 128), BF16),
        scratch_shapes=[pltpu.VMEM((tq, 1), F32), pltpu.VMEM((tq, 1), F32), pltpu.VMEM((tq, 128), F32)],
        compiler_params=_params("parallel", "arbitrary"),
        name="fox_attention",
    )(qkv, qkv, qkv, cum_col, cum_row.reshape(h * nq, tq))


def _merge_kernel(*refs):
    y_refs, w_refs, g_refs, o_ref = refs[0:4], refs[4:8], refs[8:12], refs[12]
    acc = None
    for n in range(N_BRANCH):
        term = jax.nn.sigmoid(g_refs[n][...]) * _dot(y_refs[n][...], w_refs[n][...])
        acc = term if acc is None else acc + term
    o_ref[...] = acc.astype(o_ref.dtype)


def _merge(branches, w_branch, layer, gates, tm=1024, tn=512):
    t = branches[0].shape[0]
    d = D_MODEL
    nb = d // tn
    y_spec = pl.BlockSpec((tm, BRANCH_W), lambda i, j: (i, 0))
    w_specs = [pl.BlockSpec((None, None, BRANCH_W, tn), lambda i, j, n=n: (layer, n, 0, j)) for n in range(N_BRANCH)]
    g_specs = [pl.BlockSpec((tm, tn), lambda i, j, n=n: (i, n * nb + j)) for n in range(N_BRANCH)]
    return pl.pallas_call(
        _merge_kernel,
        grid=(t // tm, nb),
        in_specs=[y_spec] * 4 + w_specs + g_specs,
        out_specs=pl.BlockSpec((tm, tn), lambda i, j: (i, j)),
        out_shape=jax.ShapeDtypeStruct((t, d), BF16),
        compiler_params=_params("parallel", "parallel"),
        name="merge",
    )(*branches, *([w_branch] * N_BRANCH), *([gates] * N_BRANCH))


def _topk_rows(s, k, payloads=()):
    n = s.shape[0]
    row = lax.broadcasted_iota(I32, s.shape, 0).astype(F32)
    vals, rows, picked = [], [], [[] for _ in payloads]
    for _ in range(k):
        m = jnp.max(s, axis=0, keepdims=True)
        first = jnp.min(jnp.where(s == m, row, float(n)), axis=0, keepdims=True)
        sel = row == first
        vals.append(m)
        rows.append(first)
        for out, p in zip(picked, payloads):
            out.append(jnp.max(jnp.where(sel, p, -1.0), axis=0, keepdims=True))
        s = jnp.where(sel, -jnp.inf, s)
    cat = lambda xs: jnp.concatenate(xs, axis=0)
    return cat(vals), cat(rows), [cat(p) for p in picked]


_PEER_PAIRS = [(r1, r2) for r1 in range(PEER_TOPK) for r2 in range(PEER_TOPK) if (r1 + 1) * (r2 + 1) <= PEER_TOPK]


def _peer_topk_kernel(q_ref, keys_ref, a_ref, b_ref, g_ref):
    kq = PEER_TOPK
    half = PEER_DKEY // 2
    a_rows, b_rows, g_rows = [], [], []
    for hh in range(PEER_HEADS):
        top = []
        for p in range(2):
            qs = q_ref[:, (2 * hh + p) * half:(2 * hh + p + 1) * half]
            s = _dot(keys_ref[hh, p], qs, _NT)
            top.append(_topk_rows(s, kq)[:2])
        (v1, i1), (v2, i2) = top
        cand = jnp.concatenate([v1[r1:r1 + 1, :] + v2[r2:r2 + 1, :] for r1, r2 in _PEER_PAIRS], axis=0)
        key1 = jnp.concatenate([i1[r1:r1 + 1, :] for r1, _ in _PEER_PAIRS], axis=0)
        key2 = jnp.concatenate([i2[r2:r2 + 1, :] for _, r2 in _PEER_PAIRS], axis=0)
        best, _, (e1, e2) = _topk_rows(cand, kq, (key1, key2))
        ex = jnp.exp(best - best[0:1, :])
        a_rows.append(e1)
        b_rows.append(e2)
        g_rows.append(ex / jnp.sum(ex, axis=0, keepdims=True))
    a_ref[...] = jnp.concatenate(a_rows, axis=0).T.astype(I32)
    b_ref[...] = jnp.concatenate(b_rows, axis=0).T.astype(I32)
    g_ref[...] = jnp.concatenate(g_rows, axis=0).T


def _peer_topk(q, keys, tt=256):
    t = q.shape[0]
    n = PEER_HEADS * PEER_TOPK
    spec = pl.BlockSpec((tt, n), lambda i: (i, 0))
    return pl.pallas_call(
        _peer_topk_kernel,
        grid=(t // tt,),
        in_specs=[pl.BlockSpec((tt, q.shape[1]), lambda i: (i, 0)),
                  pl.BlockSpec(keys.shape, lambda i: (0, 0, 0, 0))],
        out_specs=[spec, spec, spec],
        out_shape=[jax.ShapeDtypeStruct((t, n), I32), jax.ShapeDtypeStruct((t, n), I32), jax.ShapeDtypeStruct((t, n), F32)],
        compiler_params=_params("parallel"),
        name="peer_topk",
    )(q, keys)


def _peer_weights_kernel(a_ref, b_ref, g_ref, o_ref):
    tt = a_ref.shape[0]
    nk = PEER_NKEYS
    sub = lax.broadcasted_iota(I32, (nk, a_ref.shape[1]), 0)

    group = 8

    def body(i, carry):
        t0 = pl.multiple_of(i * group, group)
        lefts, rights = [], []
        for u in range(group):
            a = a_ref[pl.ds(t0 + u, 1), :]
            b = b_ref[pl.ds(t0 + u, 1), :]
            g = g_ref[pl.ds(t0 + u, 1), :]
            g_hi = g.astype(BF16).astype(F32)
            g_lo = g - g_hi
            lefts.append(jnp.concatenate([jnp.where(sub == a, g_hi, 0.0), jnp.where(sub == a, g_lo, 0.0)], axis=1))
            onehot = jnp.where(sub == b, 1.0, 0.0)
            rights.append(jnp.concatenate([onehot, onehot], axis=1))
        o_ref[pl.ds(t0, group)] = _bdot(jnp.stack(lefts), jnp.stack(rights), _BNT)
        return carry

    lax.fori_loop(0, tt // group, body, 0)


def _peer_weights(a, b, g, tt=64):
    t, n = a.shape
    spec = pl.BlockSpec((tt, n), lambda i: (i, 0))
    return pl.pallas_call(
        _peer_weights_kernel,
        grid=(t // tt,),
        in_specs=[spec, spec, spec],
        out_specs=pl.BlockSpec((tt, PEER_NKEYS, PEER_NKEYS), lambda i: (i, 0, 0)),
        out_shape=jax.ShapeDtypeStruct((t, PEER_NKEYS, PEER_NKEYS), F32),
        compiler_params=_params("parallel"),
        name="peer_weights",
    )(a, b, g)


def _peer_act_kernel(h_ref, u_ref, w_ref, o_ref, *, ne1):
    tm = h_ref.shape[0]
    z = _dot(h_ref[...], u_ref[...], _NT)
    gelu = 0.5 * z * (1.0 + lax.erf(z * (2.0 ** -0.5)))
    w2 = w_ref.reshape(tm * ne1, PEER_NKEYS)
    w = jnp.concatenate([w2[pl.ds(j, tm, stride=ne1), :] for j in range(ne1)], axis=1)
    o_ref[...] = (gelu * w).astype(o_ref.dtype)


def _peer_act(h, u, layer, wts, tm=1024, te=1024):
    t, d = h.shape
    e = u.shape[1]
    ne1 = te // PEER_NKEYS
    return pl.pallas_call(
        functools.partial(_peer_act_kernel, ne1=ne1),
        grid=(t // tm, e // te),
        in_specs=[pl.BlockSpec((tm, d), lambda i, j: (i, 0), pipeline_mode=pl.Buffered(1)),
                  pl.BlockSpec((None, te, d), lambda i, j: (layer, j, 0)),
                  pl.BlockSpec((tm, ne1, PEER_NKEYS), lambda i, j: (i, j, 0))],
        out_specs=pl.BlockSpec((tm, te), lambda i, j: (i, j)),
        out_shape=jax.ShapeDtypeStruct((t, e), BF16),
        compiler_params=_params("parallel", "parallel"),
        name="peer_act",
    )(h, u, wts)


def kernel(x, c, w_in, w_branch, w_out, ada_w, ada_b, ada_table, hg_lb_logits, rw_mu, rw_w0, rw_w2, rw_a0, rw_a2, rw_g2, rw_k_k, rw_k_a, rw_r_k, rw_ln_w, rw_ln_b, rw_v0, rw_v1, rw_v2, fox_fb, peer_wq, peer_keys, peer_u, peer_v, final_norm_w):
    b, t, d = x.shape
    assert b == 1 and d == D_MODEL
    assert w_in.shape[2] == COL_GATE + N_BRANCH * d and COL_RW % 256 == 0 and COL_FOX % 256 == 0 and COL_FF % 256 == 0
    depth = w_in.shape[0]
    x = x.reshape(t, d)

    c8 = jnp.broadcast_to(c, (8, d))
    mods = _matmul(c8, ada_w, name="ada_mods", n_out=N_MOD * d, tm=8, tn=512, out_dtype=F32, pre="silu",
                   bias=ada_b.reshape(1, -1))[0].reshape(N_MOD, d)
    lbs = _hg_lower_bounds(hg_lb_logits)
    ret_tables = _retention_tables(t)
    w_in_t = jnp.swapaxes(w_in, 1, 2)
    w_ff_t = w_in_t[:, COL_FF:COL_GATE, :]
    u_bf, v_bf = peer_u.astype(BF16), peer_v.astype(BF16)
    zpad = lambda z, axis, n: jnp.pad(z, [(0, n - z.shape[i]) if i == axis else (0, 0) for i in range(z.ndim)])

    v_first = None
    for l in range(depth):
        m = mods + ada_table[l]
        sh1, sc1, g1, sh2, sc2, g2 = [m[i:i + 1] for i in range(N_MOD)]
        h = _rms_mod(x, 1.0 + sc1, sh1, BF16)
        p_rh = _matmul(h, w_in_t, name="proj_ret_hg", layer=l, wt_row0=0, n_out=COL_RW, tm=MM_TM, tn=MM_TN, out_dtype=F32)
        p_rw = _matmul(h, w_in_t, name="proj_rwkv", layer=l, wt_row0=COL_RW, n_out=RW_COLS, tm=MM_TM, tn=MM_TN // 2,
                       out_dtype=F32)
        p_fox = _matmul(h, w_in_t, name="proj_fox", layer=l, wt_row0=COL_FOX, n_out=COL_FF - COL_FOX, tm=MM_TM // 2,
                        tn=MM_TN, out_dtype=BF16)
        p_gate = _matmul(h, w_in_t, name="proj_gate", layer=l, wt_row0=COL_GATE, n_out=N_BRANCH * d, tm=MM_TM, tn=MM_TN,
                         out_dtype=F32)
        ya = _retention(p_rh, ret_tables)
        yb = _hgrn2(p_rh, lbs[l:l + 1])
        w2p = jnp.concatenate([rw_w2[l], jnp.zeros_like(rw_a2[l])], axis=0)
        a2p = jnp.concatenate([jnp.zeros_like(rw_w2[l]), rw_a2[l]], axis=0)
        vres = None
        if l > 0:
            vres = (v_first, rw_v0[l - 1][None], zpad(rw_v1[l - 1], 1, 128), zpad(rw_v2[l - 1], 0, 128))
        prep = _rw_prep(p_rw, rw_mu[l][None], rw_w0[l][None], w2p, rw_a0[l][None], a2p, rw_g2[l], vres)
        if l == 0:
            v_first = prep[3]
        yc = _rwkv7(prep, rw_k_k[l][None], rw_k_a[l][None], rw_r_k[l].reshape(1, RW_W), rw_ln_w[l][None], rw_ln_b[l][None])
        cum_row, cum_col = _fox_cum(h, w_ff_t[l], fox_fb[l])
        yd = _fox_attention(p_fox, cum_row, cum_col)
        merged = _merge([ya, yb, yc, yd], w_branch, l, p_gate)
        x = _matmul(merged, w_out, name="out_proj", layer=l, n_out=d, tm=MM_TM, tn=MM_TN, out_dtype=F32, res=x, gate=g1)
        h2 = _rms_mod(x, 1.0 + sc2, sh2, BF16)
        q = _matmul(h2, peer_wq, name="peer_query", layer=l, n_out=PEER_HEADS * PEER_DKEY, tm=MM_TM // 2, tn=MM_TN,
                    out_dtype=F32)
        ea, eb, eg = _peer_topk(q, peer_keys[l])
        wts = _peer_weights(ea, eb, eg)
        act = _peer_act(h2, u_bf, l, wts)
        x = _matmul_acc_res(act, v_bf, l, x, g2, tm=1024, tn=1024, tk=2048)
    out = _rms_mod(x, final_norm_w[None], jnp.zeros((1, d), F32), F32)
    return out.reshape(b, t, d)
```

```python
import functools

import jax
import jax.numpy as jnp
import numpy as np
from jax import lax
from jax.experimental import pallas as pl
from jax.experimental.pallas import tpu as pltpu

F32, BF16, I32 = jnp.float32, jnp.bfloat16, jnp.int32

D_MODEL = 4096
NORM_EPS = 1e-6
N_MOD = 6
RET_HEADS, RET_DK, RET_CHUNK, ROPE_BASE = 8, 128, 128, 10000.0
HG_HEADS, HG_DK = 8, 128
HG_SUB = 16
RW_HEADS, RW_HD = 16, 64
RW_W = RW_HEADS * RW_HD
RW_CHUNK = 64
RW_LN_EPS = 64e-5
FOX_HEADS, FOX_HD = 8, 128
N_BRANCH, BRANCH_W = 4, 1024
PEER_HEADS, PEER_NKEYS, PEER_DKEY, PEER_TOPK = 8, 128, 256, 16
PEER_EXPERTS = PEER_NKEYS * PEER_NKEYS
COL_RW = 8192
RW_COLS = 3 * RW_W + 64 + 64 + 128
COL_FOX = COL_RW + RW_COLS
COL_FF = COL_FOX + 3 * FOX_HEADS * FOX_HD
COL_GATE = COL_FF + FOX_HEADS

V7X_VMEM_LIMIT_BYTES = 56 * 1024 * 1024
MM_TM, MM_TN = 2048, 512

_NN = (((1,), (0,)), ((), ()))
_NT = (((1,), (1,)), ((), ()))
_TN = (((0,), (0,)), ((), ()))


def _dot(a, b, dims=_NN):
    return lax.dot_general(a.astype(BF16), b.astype(BF16), dims, preferred_element_type=F32)


_BNN = (((2,), (1,)), ((0,), (0,)))
_BNT = (((2,), (2,)), ((0,), (0,)))
_BTN = (((1,), (1,)), ((0,), (0,)))


def _block_tril(size, block):
    ti = lax.broadcasted_iota(I32, (size, size), 0)
    si = lax.broadcasted_iota(I32, (size, size), 1)
    return ((ti >= si) & (ti // block == si // block)).astype(F32)


def _dot_sel(sel, x, x_first=False):
    hi = x.astype(BF16)
    r1 = x - hi.astype(F32)
    mid = r1.astype(BF16)
    lo = (r1 - mid.astype(F32)).astype(BF16)
    s = sel.astype(BF16)
    out = None
    for part in (hi, mid, lo):
        ab = (part, s) if x_first else (s, part)
        term = lax.dot_general(*ab, _NN, preferred_element_type=F32)
        out = term if out is None else out + term
    return out


def _params(*sem):
    return pltpu.CompilerParams(dimension_semantics=sem, vmem_limit_bytes=V7X_VMEM_LIMIT_BYTES)


def _mm_kernel(*refs, pre, has_bias, has_res, w_transposed):
    a_ref, w_ref = refs[0], refs[1]
    k = 2
    a = a_ref[...]
    if pre == "silu":
        a = a * jax.nn.sigmoid(a)
    y = _dot(a, w_ref[0], _NT) if w_transposed else _dot(a, w_ref[...])
    if has_bias:
        y = y + refs[k][...]
        k += 1
    if has_res:
        y = refs[k][...] + refs[k + 1][...] * y
        k += 2
    refs[k][...] = y.astype(refs[k].dtype)


def _matmul(a, w, *, name, n_out, tm, tn, out_dtype, layer=None, col_block0=0, wt_row0=None, pre=None, bias=None, res=None,
            gate=None):
    m, kdim = a.shape
    assert m % tm == 0 and (wt_row0 is None or (n_out % tn == 0 and wt_row0 % 8 == 0))
    grid = (m // tm, pl.cdiv(n_out, tn))
    if wt_row0 is not None:
        w_spec = pl.BlockSpec((pl.Element(1), pl.Element(tn), pl.Element(kdim)),
                              lambda i, j: (layer, pl.multiple_of(wt_row0 + j * tn, 8), 0))
    elif layer is None:
        w_spec = pl.BlockSpec((kdim, tn), lambda i, j: (0, j + col_block0))
    else:
        w_spec = pl.BlockSpec((None, kdim, tn), lambda i, j: (layer, 0, j + col_block0))
    a_mode = pl.Buffered(1) if grid[1] >= 8 else None
    in_specs = [pl.BlockSpec((tm, kdim), lambda i, j: (i, 0), pipeline_mode=a_mode), w_spec]
    args = [a, w]
    if bias is not None:
        in_specs.append(pl.BlockSpec((1, tn), lambda i, j: (0, j)))
        args.append(bias)
    if res is not None:
        in_specs += [pl.BlockSpec((tm, tn), lambda i, j: (i, j)), pl.BlockSpec((1, tn), lambda i, j: (0, j))]
        args += [res, gate]
    return pl.pallas_call(
        functools.partial(_mm_kernel, pre=pre, has_bias=bias is not None, has_res=res is not None,
                          w_transposed=wt_row0 is not None),
        grid=grid,
        in_specs=in_specs,
        out_specs=pl.BlockSpec((tm, tn), lambda i, j: (i, j)),
        out_shape=jax.ShapeDtypeStruct((m, n_out), out_dtype),
        compiler_params=_params("parallel", "parallel"),
        name=name,
    )(*args)


def _mm_acc_kernel(a_ref, w_ref, x_ref, g_ref, o_ref, acc_ref):
    @pl.when(pl.program_id(2) == 0)
    def _():
        acc_ref[...] = jnp.zeros_like(acc_ref)

    acc_ref[...] += _dot(a_ref[...], w_ref[...])

    @pl.when(pl.program_id(2) == pl.num_programs(2) - 1)
    def _():
        o_ref[...] = x_ref[...] + g_ref[...] * acc_ref[...]


def _matmul_acc_res(a, w, layer, res, gate, *, tm, tn, tk):
    m, kdim = a.shape
    n = w.shape[2]
    assert kdim % tk == 0 and m % tm == 0 and n % tn == 0
    return pl.pallas_call(
        _mm_acc_kernel,
        grid=(m // tm, n // tn, kdim // tk),
        in_specs=[
            pl.BlockSpec((tm, tk), lambda i, j, k: (i, k)),
            pl.BlockSpec((None, tk, tn), lambda i, j, k: (layer, k, j)),
            pl.BlockSpec((tm, tn), lambda i, j, k: (i, j)),
            pl.BlockSpec((1, tn), lambda i, j, k: (0, j)),
        ],
        out_specs=pl.BlockSpec((tm, tn), lambda i, j, k: (i, j)),
        out_shape=jax.ShapeDtypeStruct((m, n), F32),
        scratch_shapes=[pltpu.VMEM((tm, tn), F32)],
        compiler_params=_params("parallel", "parallel", "arbitrary"),
        name="peer_out",
    )(a, w, res, gate)


def _norm_kernel(x_ref, mul_ref, add_ref, o_ref):
    x = x_ref[...]
    ms = jnp.mean(x * x, axis=-1, keepdims=True)
    o_ref[...] = (x * lax.rsqrt(ms + NORM_EPS) * mul_ref[...] + add_ref[...]).astype(o_ref.dtype)


def _rms_mod(x, mul, add, out_dtype, tm=256):
    m, d = x.shape
    return pl.pallas_call(
        _norm_kernel,
        grid=(m // tm,),
        in_specs=[pl.BlockSpec((tm, d), lambda i: (i, 0)), pl.BlockSpec((1, d), lambda i: (0, 0)),
                  pl.BlockSpec((1, d), lambda i: (0, 0))],
        out_specs=pl.BlockSpec((tm, d), lambda i: (i, 0)),
        out_shape=jax.ShapeDtypeStruct((m, d), out_dtype),
        compiler_params=_params("parallel"),
        name="rms_mod",
    )(x, mul, add)


def _hg_lb_kernel(lg_ref, o_ref):
    lg = lg_ref[...]
    e = jnp.exp(lg - jnp.max(lg, axis=0, keepdims=True))
    p = e / jnp.sum(e, axis=0, keepdims=True)
    rows = [jnp.zeros_like(p[0:1])]
    for l in range(1, lg.shape[0]):
        rows.append(rows[-1] + p[l:l + 1])
    o_ref[...] = jnp.concatenate(rows, axis=0)


def _hg_lower_bounds(logits):
    return pl.pallas_call(_hg_lb_kernel, out_shape=jax.ShapeDtypeStruct(logits.shape, F32), name="hg_lb")(logits)


def _ret_body(q_ref, k_ref, v_ref, g_ref, cos_ref, sin_ref, dmat_ref, xi_ref, zeta_ref, gam_ref, o_ref, st_ref, first):
    c = RET_CHUNK
    dmat, xi, zeta, gam = dmat_ref[...], xi_ref[...], zeta_ref[...], gam_ref[...]
    st = jnp.where(first, 0.0, st_ref[...])
    for ci in range(q_ref.shape[0] // c):
        sl = slice(ci * c, (ci + 1) * c)
        q, k, v = q_ref[sl, :], k_ref[sl, :], v_ref[sl, :]
        cos, sin = cos_ref[sl, :], sin_ref[sl, :]
        half = RET_DK // 2
        qr = q * cos + pltpu.roll(q, half, 1) * sin
        kr = (k * cos + pltpu.roll(k, half, 1) * sin) * (RET_DK ** -0.5)
        scores = _dot(qr, kr, _NT) * dmat
        o = _dot(scores, v) + _dot(qr * xi, st)
        st = st * gam + _dot(kr * zeta, v, _TN)
        ms = jnp.mean(o * o, axis=-1, keepdims=True)
        g = g_ref[sl, :]
        y = o * lax.rsqrt(ms + NORM_EPS) * (g * jax.nn.sigmoid(g))
        o_ref[sl, :] = y.astype(o_ref.dtype)
    st_ref[...] = st


def _ret_kernel(*refs):
    _ret_body(*refs, pl.program_id(1) == 0)


def _retention(proj, tables, tb=512):
    t = proj.shape[0]
    h = RET_HEADS
    cos, sin, dmat, xi, zeta, gam = tables
    c = RET_CHUNK
    col = lambda g: pl.BlockSpec((tb, 128), lambda hh, n, g=g: (n, g * h + hh))
    tab = pl.BlockSpec((tb, 128), lambda hh, n: (n, 0))
    return pl.pallas_call(
        _ret_kernel,
        grid=(h, t // tb),
        in_specs=[col(0), col(1), col(2), col(3), tab, tab,
                  pl.BlockSpec((None, c, c), lambda hh, n: (hh, 0, 0)),
                  pl.BlockSpec((None, c, 128), lambda hh, n: (hh, 0, 0)),
                  pl.BlockSpec((None, c, 128), lambda hh, n: (hh, 0, 0)),
                  pl.BlockSpec((None, 1, 128), lambda hh, n: (hh, 0, 0))],
        out_specs=pl.BlockSpec((tb, 128), lambda hh, n: (n, hh)),
        out_shape=jax.ShapeDtypeStruct((t, h * 128), BF16),
        scratch_shapes=[pltpu.VMEM((RET_DK, 128), F32)],
        compiler_params=_params("parallel", "arbitrary"),
        name="retention",
    )(proj, proj, proj, proj, cos, sin, dmat, xi, zeta, gam)


def _retention_tables(t):
    half = RET_DK // 2
    inv = ROPE_BASE ** (-jnp.arange(half, dtype=F32) / half)
    ang = jnp.arange(t, dtype=F32)[:, None] * inv[None, :]
    cos, sin = jnp.cos(ang), jnp.sin(ang)
    cos2 = jnp.concatenate([cos, cos], axis=-1)
    sin2 = jnp.concatenate([-sin, sin], axis=-1)
    c = RET_CHUNK
    log_g = jnp.log(1.0 - 2.0 ** (-5.0 - jnp.arange(RET_HEADS, dtype=F32)))
    pos = jnp.arange(c, dtype=F32)
    rel = pos[:, None] - pos[None, :]
    dmat = jnp.where(rel >= 0, jnp.exp(log_g[:, None, None] * jnp.maximum(rel, 0.0)), 0.0)
    xi = jnp.broadcast_to(jnp.exp(log_g[:, None] * (pos + 1.0))[:, :, None], (RET_HEADS, c, 128))
    zeta = jnp.broadcast_to(jnp.exp(log_g[:, None] * (c - 1.0 - pos))[:, :, None], (RET_HEADS, c, 128))
    gam = jnp.broadcast_to(jnp.exp(log_g * c)[:, None, None], (RET_HEADS, 1, 128))
    return cos2, sin2, dmat, xi, zeta, gam


def _hg_body(q_ref, f_ref, v_ref, g_ref, lb_ref, o_ref, st_ref, first):
    n = HG_SUB
    tb = q_ref.shape[0]
    nb = tb // n
    lb = lb_ref[...]
    fg = lb + (1.0 - lb) * jax.nn.sigmoid(f_ref[...])
    kk = (1.0 - fg).reshape(nb, n, 128)
    cum = _dot_sel(_block_tril(tb, n), jnp.log(fg)).reshape(nb, n, 128)
    q, v = q_ref[...].reshape(nb, n, 128), v_ref[...].reshape(nb, n, 128)
    row = lax.broadcasted_iota(I32, (nb, n, 128), 1)
    o = jnp.zeros((nb, n, 128), F32)
    for s in range(n):
        d = jnp.where(row >= s, cum - cum[:, s:s + 1, :], -jnp.inf)
        col = jnp.sum(q * jnp.exp(d) * kk[:, s:s + 1, :], axis=-1, keepdims=True)
        o = o + col * v[:, s:s + 1, :]
    last = cum[:, n - 1:n, :]
    upd = lax.dot_general(v.astype(BF16), (kk * jnp.exp(last - cum)).astype(BF16), _BTN, preferred_element_type=F32)
    dec = jnp.exp(last)
    st = jnp.where(first, 0.0, st_ref[...])
    entering = []
    for j in range(nb):
        entering.append(st)
        st = st * dec[j] + upd[j]
    st_ref[...] = st
    o = o + lax.dot_general((q * jnp.exp(cum)).astype(BF16), jnp.stack(entering).astype(BF16), _BNT,
                            preferred_element_type=F32)
    o = o.reshape(tb, 128)
    ms = jnp.mean(o * o, axis=-1, keepdims=True)
    g = g_ref[...]
    o_ref[...] = (o * lax.rsqrt(ms + NORM_EPS) * (g * jax.nn.sigmoid(g))).astype(o_ref.dtype)


def _hg_kernel(*refs):
    _hg_body(*refs, pl.program_id(1) == 0)


def _hgrn2(proj, lb, tb=256):
    t = proj.shape[0]
    h = HG_HEADS
    col = lambda g: pl.BlockSpec((tb, 128), lambda hh, n, g=g: (n, (4 + g) * h + hh))
    return pl.pallas_call(
        _hg_kernel,
        grid=(h, t // tb),
        in_specs=[col(0), col(1), col(2), col(3), pl.BlockSpec((1, 128), lambda hh, n: (0, hh))],
        out_specs=pl.BlockSpec((tb, 128), lambda hh, n: (n, hh)),
        out_shape=jax.ShapeDtypeStruct((t, h * 128), BF16),
        scratch_shapes=[pltpu.VMEM((128, HG_DK), F32)],
        compiler_params=_params("parallel", "arbitrary"),
        name="hgrn2",
    )(proj, proj, proj, proj, lb)


def _rw_prep_values(cur_ref, prv_ref, mu_ref, w0_ref, w2_ref, a0_ref, a2_ref, g2_ref, vres_refs):
    cur = cur_ref[...]
    prev_row = jnp.where(pl.program_id(0) == 0, 0.0, prv_ref[7:8, :])
    row = lax.broadcasted_iota(I32, cur.shape, 0)
    shifted = jnp.where(row == 0, prev_row, pltpu.roll(cur, 1, 0))
    cols = cur + (shifted - cur) * mu_ref[...]
    w = RW_W
    r, k, v = cols[:, 0:w], cols[:, w:2 * w], cols[:, 2 * w:3 * w]
    wa = cols[:, 3 * w:3 * w + 128]
    lane = lax.broadcasted_iota(I32, wa.shape, 1)
    wa = jnp.where(lane < 64, jnp.tanh(wa), wa)
    gl = cols[:, 3 * w + 128:3 * w + 256]
    wlog = -jax.nn.softplus(-(w0_ref[...] + _dot(wa, w2_ref[...]))) - 0.5
    a = jax.nn.sigmoid(a0_ref[...] + _dot(wa, a2_ref[...]))
    if vres_refs is not None:
        vf_ref, v0_ref, v1_ref, v2_ref = vres_refs
        mix = jax.nn.sigmoid(v0_ref[...] + _dot(_dot(v, v1_ref[...]), v2_ref[...]))
        v = v + (vf_ref[...] - v) * mix
    ld = -jnp.exp(wlog)
    return r, ld, k, v, a, _dot(jax.nn.sigmoid(gl), g2_ref[...])


def _rw_level_masks(c):
    ti = lax.broadcasted_iota(I32, (c, c), 0)
    si = lax.broadcasted_iota(I32, (c, c), 1)
    masks = []
    n = 1
    while n < c:
        same = (ti // (2 * n)) == (si // (2 * n))
        masks.append(same & ((ti // n) % 2 == 1) & ((si // n) % 2 == 0))
        n *= 2
    return ti, si, masks


def _bdot(a, b, dims=_BNN):
    return lax.dot_general(a.astype(BF16), b.astype(BF16), dims, preferred_element_type=F32)


def _rw_kernel(*refs, tb, has_vres):
    n_prep = 12 if has_vres else 8
    vres_refs = refs[8:12] if has_vres else None
    kk_ref, ka_ref, rk_ref, lnw_ref, lnb_ref = refs[n_prep:n_prep + 5]
    outs = refs[n_prep + 5:]
    o_ref, st_ref = outs[0], outs[-1]
    r2d, ld2, k2d, v2d, a2d, g2d = _rw_prep_values(*refs[:8], vres_refs)
    if not has_vres:
        outs[1][...] = v2d
    c, nh, hd = RW_CHUNK, RW_HEADS, RW_HD
    nc = tb // c

    @pl.when(pl.program_id(0) == 0)
    def _():
        st_ref[...] = jnp.zeros_like(st_ref)

    def split(x):
        return jnp.stack([x[ci * c:(ci + 1) * c, h * hd:(h + 1) * hd] for ci in range(nc) for h in range(nh)])

    def split_row(ref):
        x = ref[...]
        return jnp.stack([x[:, h * hd:(h + 1) * hd] for _ in range(nc) for h in range(nh)])

    cw2 = _dot_sel(_block_tril(tb, c), ld2)
    r, ld, cw, k, v, a = split(r2d), split(ld2), split(cw2), split(k2d), split(v2d), split(a2d)
    kk_w, ka_w, rk_w, lnw, lnb = (split_row(p) for p in (kk_ref, ka_ref, rk_ref, lnw_ref, lnb_ref))
    ti, si, masks = _rw_level_masks(c)
    tril, strict = (ti >= si)[None], (ti > si)[None]

    wc = jnp.exp(cw)
    wprev = jnp.exp(cw - ld)
    winv = jnp.exp(-cw)
    wc_last = wc[:, c - 1:c, :]
    wtail = wc_last * winv
    kkf = k * kk_w
    kk = kkf / jnp.maximum(jnp.sqrt(jnp.sum(kkf * kkf, axis=-1, keepdims=True)), 1e-12)
    k2 = k * (1.0 + (a - 1.0) * ka_w)
    bv = kk * a
    at = -kk * wprev
    rt = r * wc
    ar = jnp.concatenate([at, rt], axis=1)
    gb = _bdot(ar, bv * winv, _BNT)
    gk = _bdot(ar, k2 * winv, _BNT)
    lab = jnp.where(strict, gb[:, :c], 0.0)
    lak = jnp.where(strict, gk[:, :c], 0.0)
    mrb = jnp.where(tril, gb[:, c:], 0.0)
    mrk = jnp.where(tril, gk[:, c:], 0.0)
    x = jnp.where((ti == si)[None], 1.0, 0.0) + jnp.where(masks[0][None], lab, 0.0)
    for m in masks[1:]:
        x = x + _bdot(_bdot(x, jnp.where(m[None], lab, 0.0)), x)
    lv = _bdot(jnp.concatenate([lak, mrk], axis=1), v)
    p = _bdot(x, at)
    q = _bdot(x, lv[:, :c])
    r2 = rt + _bdot(mrb, p)
    y0 = _bdot(mrb, q) + lv[:, c:]
    bh = bv * wtail
    g = _bdot(p, bh, _BTN)
    s_add = _bdot(q, bh, _BTN) + _bdot(v, k2 * wtail, _BTN)
    s = st_ref[...]
    entering = []
    for ci in range(nc):
        b0, b1 = ci * nh, (ci + 1) * nh
        entering.append(s)
        s = s * wc_last[b0:b1] + _bdot(s, g[b0:b1]) + s_add[b0:b1]
    st_ref[...] = s
    y = _bdot(r2, jnp.concatenate(entering, axis=0), _BNT) + y0
    mean = jnp.mean(y, axis=-1, keepdims=True)
    var = jnp.mean(jnp.square(y - mean), axis=-1, keepdims=True)
    yn = (y - mean) * lax.rsqrt(var + RW_LN_EPS) * lnw + lnb
    out = yn + jnp.sum(r * k2 * rk_w, axis=-1, keepdims=True) * v
    for ci in range(nc):
        rows = jnp.concatenate([out[ci * nh + h] for h in range(nh)], axis=1)
        o_ref[ci * c:(ci + 1) * c, :] = (rows * g2d[ci * c:(ci + 1) * c, :]).astype(o_ref.dtype)


def _rwkv7(cols, mu, w0, w2p, a0, a2p, g2, vres, k_k, k_a, r_k, ln_w, ln_b, tb=256):
    t, nc = cols.shape
    w = RW_W
    full = lambda shape: pl.BlockSpec(shape, lambda i: (0,) * len(shape))
    col = pl.BlockSpec((tb, w), lambda i: (i, 0))
    in_specs = [pl.BlockSpec((tb, nc), lambda i: (i, 0)),
                pl.BlockSpec((8, nc), lambda i: (jnp.maximum(i * (tb // 8) - 1, 0), 0)),
                full((1, nc)), full((1, w)), full((128, w)), full((1, w)), full((128, w)), full((128, w))]
    args = [cols, cols, mu, w0, w2p, a0, a2p, g2]
    if vres is not None:
        v_first, v0, v1p, v2p = vres
        in_specs += [col, full((1, w)), full((w, 128)), full((128, w))]
        args += [v_first, v0, v1p, v2p]
    in_specs += [full((1, w))] * 5
    args += [k_k, k_a, r_k, ln_w, ln_b]
    out_specs, out_shape = [col], [jax.ShapeDtypeStruct((t, w), BF16)]
    if vres is None:
        out_specs.append(col)
        out_shape.append(jax.ShapeDtypeStruct((t, w), F32))
    res = pl.pallas_call(
        functools.partial(_rw_kernel, tb=tb, has_vres=vres is not None),
        grid=(t // tb,),
        in_specs=in_specs,
        out_specs=out_specs,
        out_shape=out_shape,
        scratch_shapes=[pltpu.VMEM((RW_HEADS, RW_HD, RW_HD), F32)],
        compiler_params=_params("arbitrary"),
        name="rwkv7",
    )(*args)
    return (res[0], res[1]) if vres is None else (res[0], None)


def _fox_cum_kernel(h_ref, wt_ref, w_ref, fbc_ref, fbr_ref, row_ref, col_ref, crow_ref, ccol_ref):
    @pl.when(pl.program_id(0) == 0)
    def _():
        crow_ref[...] = jnp.zeros_like(crow_ref)
        ccol_ref[...] = jnp.zeros_like(ccol_ref)

    h = h_ref[...]
    tt = h.shape[0]
    ti = lax.broadcasted_iota(I32, (tt, tt), 0)
    si = lax.broadcasted_iota(I32, (tt, tt), 1)
    lf_r = jax.nn.log_sigmoid(_dot(wt_ref[...], h, _NT) + fbc_ref[...])
    cum_r = _dot_sel((ti <= si).astype(F32), lf_r, x_first=True) + crow_ref[:, 0:1]
    row_ref[...] = cum_r
    crow_ref[...] = jnp.broadcast_to(cum_r[:, tt - 1:tt], crow_ref.shape)
    lf_c = jax.nn.log_sigmoid(_dot(h, w_ref[...]) + fbr_ref[...])
    cum_c = _dot_sel((ti >= si).astype(F32), lf_c) + ccol_ref[0:1, :]
    col_ref[...] = cum_c
    ccol_ref[...] = jnp.broadcast_to(cum_c[tt - 1:tt, :], ccol_ref.shape)


def _fox_cum(h, w_ff_t, fb, tt=512):
    t, d = h.shape
    nh = FOX_HEADS
    full = lambda shape: pl.BlockSpec(shape, lambda i: (0,) * len(shape))
    return pl.pallas_call(
        _fox_cum_kernel,
        grid=(t // tt,),
        in_specs=[pl.BlockSpec((tt, d), lambda i: (i, 0)), full((nh, d)), full((d, nh)), full((nh, 1)), full((1, nh))],
        out_specs=[pl.BlockSpec((nh, tt), lambda i: (0, i)), pl.BlockSpec((tt, nh), lambda i: (i, 0))],
        out_shape=[jax.ShapeDtypeStruct((nh, t), F32), jax.ShapeDtypeStruct((t, nh), F32)],
        scratch_shapes=[pltpu.VMEM((nh, 128), F32), pltpu.VMEM((8, nh), F32)],
        compiler_params=_params("arbitrary"),
        name="fox_cum",
    )(h, w_ff_t, w_ff_t.T, fb.reshape(nh, 1), fb.reshape(1, nh))


def _fox_kernel(q_ref, k_ref, v_ref, cq_ref, ck_ref, o_ref, m_ref, l_ref, acc_ref, *, tq):
    hh, qi = pl.program_id(0), pl.program_id(1)
    nk = k_ref.shape[0] // tq
    q = q_ref[...]
    lane = lax.broadcasted_iota(I32, cq_ref.shape, 1)
    cq = jnp.sum(jnp.where(lane == hh, cq_ref[...], 0.0), axis=-1, keepdims=True)
    m_ref[...] = jnp.full_like(m_ref, -jnp.inf)
    l_ref[...] = jnp.zeros_like(l_ref)
    acc_ref[...] = jnp.zeros_like(acc_ref)

    log2e = 1.4426950408889634
    cq2 = cq * log2e

    def block(ki, diagonal):
        rows = pl.ds(pl.multiple_of(ki * tq, tq), tq)
        s = _dot(q, k_ref[rows, :], _NT) * (FOX_HD ** -0.5 * log2e) - ck_ref[pl.ds(hh * nk + ki, 1), :] * log2e
        if diagonal:
            s = jnp.where(lax.broadcasted_iota(I32, s.shape, 1) <= lax.broadcasted_iota(I32, s.shape, 0), s, -jnp.inf)
        m_old = m_ref[...]
        m_new = jnp.maximum(m_old, jnp.max(s, axis=-1, keepdims=True) + cq2)
        alpha = jnp.exp2(m_old - m_new)
        p = jnp.exp2(s - (m_new - cq2))
        l_ref[...] = alpha * l_ref[...] + jnp.sum(p, axis=-1, keepdims=True)
        acc_ref[...] = alpha * acc_ref[...] + _dot(p, v_ref[rows, :])
        m_ref[...] = m_new

    def body(ki, carry):
        block(ki, False)
        return carry

    lax.fori_loop(0, qi, body, 0)
    block(qi, True)
    o_ref[...] = (acc_ref[...] / l_ref[...]).astype(o_ref.dtype)


def _fox_attention(qkv, cum_row, cum_col, tq=2048):
    t = qkv.shape[0]
    h = FOX_HEADS
    nq = t // tq
    return pl.pallas_call(
        functools.partial(_fox_kernel, tq=tq),
        grid=(h, nq),
        in_specs=[pl.BlockSpec((tq, 128), lambda hh, qi: (qi, hh)),
                  pl.BlockSpec((t, 128), lambda hh, qi: (0, h + hh)),
                  pl.BlockSpec((t, 128), lambda hh, qi: (0, 2 * h + hh)),
                  pl.BlockSpec((tq, h), lambda hh, qi: (qi, 0)),
                  pl.BlockSpec((h * nq, tq), lambda hh, qi: (0, 0))],
        out_specs=pl.BlockSpec((tq, 128), lambda hh, qi: (qi, hh)),
        out_shape=jax.ShapeDtypeStruct((t, h * 128), BF16),
        scratch_shapes=[pltpu.VMEM((tq, 1), F32), pltpu.VMEM((tq, 1), F32), pltpu.VMEM((tq, 128), F32)],
        compiler_params=_params("parallel", "arbitrary"),
        name="fox_attention",
    )(qkv, qkv, qkv, cum_col, cum_row.reshape(h * nq, tq))


def _merge_kernel(*refs):
    y_refs, w_refs, g_refs, o_ref = refs[0:4], refs[4:8], refs[8:12], refs[12]
    acc = None
    for n in range(N_BRANCH):
        term = jax.nn.sigmoid(g_refs[n][...]) * _dot(y_refs[n][...], w_refs[n][...])
        acc = term if acc is None else acc + term
    o_ref[...] = acc.astype(o_ref.dtype)


def _merge(branches, w_branch, layer, gates, tm=1024, tn=512):
    t = branches[0].shape[0]
    d = D_MODEL
    nb = d // tn
    y_spec = pl.BlockSpec((tm, BRANCH_W), lambda i, j: (i, 0))
    w_specs = [pl.BlockSpec((None, None, BRANCH_W, tn), lambda i, j, n=n: (layer, n, 0, j)) for n in range(N_BRANCH)]
    g_specs = [pl.BlockSpec((tm, tn), lambda i, j, n=n: (i, n * nb + j)) for n in range(N_BRANCH)]
    return pl.pallas_call(
        _merge_kernel,
        grid=(t // tm, nb),
        in_specs=[y_spec] * 4 + w_specs + g_specs,
        out_specs=pl.BlockSpec((tm, tn), lambda i, j: (i, j)),
        out_shape=jax.ShapeDtypeStruct((t, d), BF16),
        compiler_params=_params("parallel", "parallel"),
        name="merge",
    )(*branches, *([w_branch] * N_BRANCH), *([gates] * N_BRANCH))


def _topk_rows(s, k, payloads=()):
    n = s.shape[0]
    row = lax.broadcasted_iota(I32, s.shape, 0).astype(F32)
    vals, rows, picked = [], [], [[] for _ in payloads]
    for _ in range(k):
        m = jnp.max(s, axis=0, keepdims=True)
        first = jnp.min(jnp.where(s == m, row, float(n)), axis=0, keepdims=True)
        sel = row == first
        vals.append(m)
        rows.append(first)
        for out, p in zip(picked, payloads):
            out.append(jnp.max(jnp.where(sel, p, -1.0), axis=0, keepdims=True))
        s = jnp.where(sel, -jnp.inf, s)
    cat = lambda xs: jnp.concatenate(xs, axis=0)
    return cat(vals), cat(rows), [cat(p) for p in picked]


_PEER_PAIRS = [(r1, r2) for r1 in range(PEER_TOPK) for r2 in range(PEER_TOPK) if (r1 + 1) * (r2 + 1) <= PEER_TOPK]


def _peer_topk_kernel(q_ref, keys_ref, a_ref, b_ref, g_ref):
    kq = PEER_TOPK
    half = PEER_DKEY // 2
    a_rows, b_rows, g_rows = [], [], []
    for hh in range(PEER_HEADS):
        top = []
        for p in range(2):
            qs = q_ref[:, (2 * hh + p) * half:(2 * hh + p + 1) * half]
            s = _dot(keys_ref[hh, p], qs, _NT)
            top.append(_topk_rows(s, kq)[:2])
        (v1, i1), (v2, i2) = top
        cand = jnp.concatenate([v1[r1:r1 + 1, :] + v2[r2:r2 + 1, :] for r1, r2 in _PEER_PAIRS], axis=0)
        key1 = jnp.concatenate([i1[r1:r1 + 1, :] for r1, _ in _PEER_PAIRS], axis=0)
        key2 = jnp.concatenate([i2[r2:r2 + 1, :] for _, r2 in _PEER_PAIRS], axis=0)
        best, _, (e1, e2) = _topk_rows(cand, kq, (key1, key2))
        ex = jnp.exp(best - best[0:1, :])
        a_rows.append(e1)
        b_rows.append(e2)
        g_rows.append(ex / jnp.sum(ex, axis=0, keepdims=True))
    a_ref[...] = jnp.concatenate(a_rows, axis=0).T.astype(I32)
    b_ref[...] = jnp.concatenate(b_rows, axis=0).T.astype(I32)
    g_ref[...] = jnp.concatenate(g_rows, axis=0).T


def _peer_topk(q, keys, tt=256):
    t = q.shape[0]
    n = PEER_HEADS * PEER_TOPK
    spec = pl.BlockSpec((tt, n), lambda i: (i, 0))
    return pl.pallas_call(
        _peer_topk_kernel,
        grid=(t // tt,),
        in_specs=[pl.BlockSpec((tt, q.shape[1]), lambda i: (i, 0)),
                  pl.BlockSpec(keys.shape, lambda i: (0, 0, 0, 0))],
        out_specs=[spec, spec, spec],
        out_shape=[jax.ShapeDtypeStruct((t, n), I32), jax.ShapeDtypeStruct((t, n), I32), jax.ShapeDtypeStruct((t, n), F32)],
        compiler_params=_params("parallel"),
        name="peer_topk",
    )(q, keys)


def _peer_weights_kernel(a_ref, b_ref, g_ref, o_ref):
    tt = a_ref.shape[0]
    nk = PEER_NKEYS
    sub = lax.broadcasted_iota(I32, (nk, a_ref.shape[1]), 0)

    group = 8

    def body(i, carry):
        t0 = pl.multiple_of(i * group, group)
        lefts, rights = [], []
        for u in range(group):
            a = a_ref[pl.ds(t0 + u, 1), :]
            b = b_ref[pl.ds(t0 + u, 1), :]
            g = g_ref[pl.ds(t0 + u, 1), :]
            g_hi = g.astype(BF16).astype(F32)
            g_lo = g - g_hi
            lefts.append(jnp.concatenate([jnp.where(sub == a, g_hi, 0.0), jnp.where(sub == a, g_lo, 0.0)], axis=1))
            onehot = jnp.where(sub == b, 1.0, 0.0)
            rights.append(jnp.concatenate([onehot, onehot], axis=1))
        o_ref[pl.ds(t0, group)] = _bdot(jnp.stack(lefts), jnp.stack(rights), _BNT)
        return carry

    lax.fori_loop(0, tt // group, body, 0)


def _peer_weights(a, b, g, tt=64):
    t, n = a.shape
    spec = pl.BlockSpec((tt, n), lambda i: (i, 0))
    return pl.pallas_call(
        _peer_weights_kernel,
        grid=(t // tt,),
        in_specs=[spec, spec, spec],
        out_specs=pl.BlockSpec((tt, PEER_NKEYS, PEER_NKEYS), lambda i: (i, 0, 0)),
        out_shape=jax.ShapeDtypeStruct((t, PEER_NKEYS, PEER_NKEYS), F32),
        compiler_params=_params("parallel"),
        name="peer_weights",
    )(a, b, g)


def _peer_act_kernel(h_ref, u_ref, w_ref, o_ref, *, ne1):
    tm = h_ref.shape[0]
    z = _dot(h_ref[...], u_ref[...], _NT)
    gelu = 0.5 * z * (1.0 + lax.erf(z * (2.0 ** -0.5)))
    w2 = w_ref.reshape(tm * ne1, PEER_NKEYS)
    w = jnp.concatenate([w2[pl.ds(j, tm, stride=ne1), :] for j in range(ne1)], axis=1)
    o_ref[...] = (gelu * w).astype(o_ref.dtype)


def _peer_act(h, u, layer, wts, tm=1024, te=1024):
    t, d = h.shape
    e = u.shape[1]
    ne1 = te // PEER_NKEYS
    return pl.pallas_call(
        functools.partial(_peer_act_kernel, ne1=ne1),
        grid=(t // tm, e // te),
        in_specs=[pl.BlockSpec((tm, d), lambda i, j: (i, 0), pipeline_mode=pl.Buffered(1)),
                  pl.BlockSpec((None, te, d), lambda i, j: (layer, j, 0)),
                  pl.BlockSpec((tm, ne1, PEER_NKEYS), lambda i, j: (i, j, 0))],
        out_specs=pl.BlockSpec((tm, te), lambda i, j: (i, j)),
        out_shape=jax.ShapeDtypeStruct((t, e), BF16),
        compiler_params=_params("parallel", "parallel"),
        name="peer_act",
    )(h, u, wts)


def kernel(x, c, w_in, w_branch, w_out, ada_w, ada_b, ada_table, hg_lb_logits, rw_mu, rw_w0, rw_w2, rw_a0, rw_a2, rw_g2, rw_k_k, rw_k_a, rw_r_k, rw_ln_w, rw_ln_b, rw_v0, rw_v1, rw_v2, fox_fb, peer_wq, peer_keys, peer_u, peer_v, final_norm_w):
    b, t, d = x.shape
    assert b == 1 and d == D_MODEL
    assert w_in.shape[2] == COL_GATE + N_BRANCH * d and COL_RW % 256 == 0 and COL_FOX % 256 == 0 and COL_FF % 256 == 0
    depth = w_in.shape[0]
    x = x.reshape(t, d)

    c8 = jnp.broadcast_to(c, (8, d))
    mods = _matmul(c8, ada_w, name="ada_mods", n_out=N_MOD * d, tm=8, tn=512, out_dtype=F32, pre="silu",
                   bias=ada_b.reshape(1, -1))[0].reshape(N_MOD, d)
    lbs = _hg_lower_bounds(hg_lb_logits)
    ret_tables = _retention_tables(t)
    w_in_t = jnp.swapaxes(w_in, 1, 2)
    w_ff_t = w_in_t[:, COL_FF:COL_GATE, :]
    u_bf, v_bf = peer_u.astype(BF16), peer_v.astype(BF16)
    zpad = lambda z, axis, n: jnp.pad(z, [(0, n - z.shape[i]) if i == axis else (0, 0) for i in range(z.ndim)])

    v_first = None
    for l in range(depth):
        m = mods + ada_table[l]
        sh1, sc1, g1, sh2, sc2, g2 = [m[i:i + 1] for i in range(N_MOD)]
        h = _rms_mod(x, 1.0 + sc1, sh1, BF16)
        p_rh = _matmul(h, w_in_t, name="proj_ret_hg", layer=l, wt_row0=0, n_out=COL_RW, tm=MM_TM, tn=MM_TN, out_dtype=F32)
        p_rw = _matmul(h, w_in_t, name="proj_rwkv", layer=l, wt_row0=COL_RW, n_out=RW_COLS, tm=MM_TM, tn=MM_TN // 2,
                       out_dtype=F32)
        p_fox = _matmul(h, w_in_t, name="proj_fox", layer=l, wt_row0=COL_FOX, n_out=COL_FF - COL_FOX, tm=MM_TM // 2,
                        tn=MM_TN, out_dtype=BF16)
        p_gate = _matmul(h, w_in_t, name="proj_gate", layer=l, wt_row0=COL_GATE, n_out=N_BRANCH * d, tm=MM_TM, tn=MM_TN,
                         out_dtype=F32)
        ya = _retention(p_rh, ret_tables)
        yb = _hgrn2(p_rh, lbs[l:l + 1])
        w2p = jnp.concatenate([rw_w2[l], jnp.zeros_like(rw_a2[l])], axis=0)
        a2p = jnp.concatenate([jnp.zeros_like(rw_w2[l]), rw_a2[l]], axis=0)
        vres = None
        if l > 0:
            vres = (v_first, rw_v0[l - 1][None], zpad(rw_v1[l - 1], 1, 128), zpad(rw_v2[l - 1], 0, 128))
        yc, v_own = _rwkv7(p_rw, rw_mu[l][None], rw_w0[l][None], w2p, rw_a0[l][None], a2p, rw_g2[l], vres, rw_k_k[l][None],
                           rw_k_a[l][None], rw_r_k[l].reshape(1, RW_W), rw_ln_w[l][None], rw_ln_b[l][None])
        if l == 0:
            v_first = v_own
        cum_row, cum_col = _fox_cum(h, w_ff_t[l], fox_fb[l])
        yd = _fox_attention(p_fox, cum_row, cum_col)
        merged = _merge([ya, yb, yc, yd], w_branch, l, p_gate)
        x = _matmul(merged, w_out, name="out_proj", layer=l, n_out=d, tm=MM_TM, tn=MM_TN, out_dtype=F32, res=x, gate=g1)
        h2 = _rms_mod(x, 1.0 + sc2, sh2, BF16)
        q = _matmul(h2, peer_wq, name="peer_query", layer=l, n_out=PEER_HEADS * PEER_DKEY, tm=MM_TM // 2, tn=MM_TN,
                    out_dtype=F32)
        ea, eb, eg = _peer_topk(q, peer_keys[l])
        wts = _peer_weights(ea, eb, eg)
        act = _peer_act(h2, u_bf, l, wts)
        x = _matmul_acc_res(act, v_bf, l, x, g2, tm=1024, tn=1024, tk=2048)
    out = _rms_mod(x, final_norm_w[None], jnp.zeros((1, d), F32), F32)
    return out.reshape(b, t, d)
```

```python
import functools

import jax
import jax.numpy as jnp
import numpy as np
from jax import lax
from jax.experimental import pallas as pl
from jax.experimental.pallas import tpu as pltpu

F32, BF16, I32 = jnp.float32, jnp.bfloat16, jnp.int32

D_MODEL = 4096
NORM_EPS = 1e-6
N_MOD = 6
RET_HEADS, RET_DK, RET_CHUNK, ROPE_BASE = 8, 128, 128, 10000.0
HG_HEADS, HG_DK = 8, 128
HG_SUB = 16
RW_HEADS, RW_HD = 16, 64
RW_W = RW_HEADS * RW_HD
RW_CHUNK = 64
RW_LN_EPS = 64e-5
FOX_HEADS, FOX_HD = 8, 128
N_BRANCH, BRANCH_W = 4, 1024
PEER_HEADS, PEER_NKEYS, PEER_DKEY, PEER_TOPK = 8, 128, 256, 16
PEER_EXPERTS = PEER_NKEYS * PEER_NKEYS
COL_RW = 8192
RW_COLS = 3 * RW_W + 64 + 64 + 128
COL_FOX = COL_RW + RW_COLS
COL_FF = COL_FOX + 3 * FOX_HEADS * FOX_HD
COL_GATE = COL_FF + FOX_HEADS

V7X_VMEM_LIMIT_BYTES = 56 * 1024 * 1024
MM_TM, MM_TN = 2048, 512

_NN = (((1,), (0,)), ((), ()))
_NT = (((1,), (1,)), ((), ()))
_TN = (((0,), (0,)), ((), ()))


def _dot(a, b, dims=_NN):
    return lax.dot_general(a.astype(BF16), b.astype(BF16), dims, preferred_element_type=F32)


_BNN = (((2,), (1,)), ((0,), (0,)))
_BNT = (((2,), (2,)), ((0,), (0,)))
_BTN = (((1,), (1,)), ((0,), (0,)))


def _block_tril(size, block):
    ti = lax.broadcasted_iota(I32, (size, size), 0)
    si = lax.broadcasted_iota(I32, (size, size), 1)
    return ((ti >= si) & (ti // block == si // block)).astype(F32)


def _dot_sel(sel, x, x_first=False):
    hi = x.astype(BF16)
    r1 = x - hi.astype(F32)
    mid = r1.astype(BF16)
    lo = (r1 - mid.astype(F32)).astype(BF16)
    s = sel.astype(BF16)
    out = None
    for part in (hi, mid, lo):
        ab = (part, s) if x_first else (s, part)
        term = lax.dot_general(*ab, _NN, preferred_element_type=F32)
        out = term if out is None else out + term
    return out


def _params(*sem):
    return pltpu.CompilerParams(dimension_semantics=sem, vmem_limit_bytes=V7X_VMEM_LIMIT_BYTES)


def _mm_kernel(*refs, pre, has_bias, has_res, w_transposed):
    a_ref, w_ref = refs[0], refs[1]
    k = 2
    a = a_ref[...]
    if pre == "silu":
        a = a * jax.nn.sigmoid(a)
    y = _dot(a, w_ref[0], _NT) if w_transposed else _dot(a, w_ref[...])
    if has_bias:
        y = y + refs[k][...]
        k += 1
    if has_res:
        y = refs[k][...] + refs[k + 1][...] * y
        k += 2
    refs[k][...] = y.astype(refs[k].dtype)


def _matmul(a, w, *, name, n_out, tm, tn, out_dtype, layer=None, col_block0=0, wt_row0=None, pre=None, bias=None, res=None,
            gate=None):
    m, kdim = a.shape
    assert m % tm == 0 and (wt_row0 is None or (n_out % tn == 0 and wt_row0 % 8 == 0))
    grid = (m // tm, pl.cdiv(n_out, tn))
    if wt_row0 is not None:
        w_spec = pl.BlockSpec((pl.Element(1), pl.Element(tn), pl.Element(kdim)),
                              lambda i, j: (layer, pl.multiple_of(wt_row0 + j * tn, 8), 0))
    elif layer is None:
        w_spec = pl.BlockSpec((kdim, tn), lambda i, j: (0, j + col_block0))
    else:
        w_spec = pl.BlockSpec((None, kdim, tn), lambda i, j: (layer, 0, j + col_block0))
    a_mode = pl.Buffered(1) if grid[1] >= 8 else None
    in_specs = [pl.BlockSpec((tm, kdim), lambda i, j: (i, 0), pipeline_mode=a_mode), w_spec]
    args = [a, w]
    if bias is not None:
        in_specs.append(pl.BlockSpec((1, tn), lambda i, j: (0, j)))
        args.append(bias)
    if res is not None:
        in_specs += [pl.BlockSpec((tm, tn), lambda i, j: (i, j)), pl.BlockSpec((1, tn), lambda i, j: (0, j))]
        args += [res, gate]
    return pl.pallas_call(
        functools.partial(_mm_kernel, pre=pre, has_bias=bias is not None, has_res=res is not None,
                          w_transposed=wt_row0 is not None),
        grid=grid,
        in_specs=in_specs,
        out_specs=pl.BlockSpec((tm, tn), lambda i, j: (i, j)),
        out_shape=jax.ShapeDtypeStruct((m, n_out), out_dtype),
        compiler_params=_params("parallel", "parallel"),
        name=name,
    )(*args)


def _mm_acc_kernel(a_ref, w_ref, x_ref, g_ref, o_ref, acc_ref):
    @pl.when(pl.program_id(2) == 0)
    def _():
        acc_ref[...] = jnp.zeros_like(acc_ref)

    acc_ref[...] += _dot(a_ref[...], w_ref[...])

    @pl.when(pl.program_id(2) == pl.num_programs(2) - 1)
    def _():
        o_ref[...] = x_ref[...] + g_ref[...] * acc_ref[...]


def _matmul_acc_res(a, w, layer, res, gate, *, tm, tn, tk):
    m, kdim = a.shape
    n = w.shape[2]
    assert kdim % tk == 0 and m % tm == 0 and n % tn == 0
    return pl.pallas_call(
        _mm_acc_kernel,
        grid=(m // tm, n // tn, kdim // tk),
        in_specs=[
            pl.BlockSpec((tm, tk), lambda i, j, k: (i, k)),
            pl.BlockSpec((None, tk, tn), lambda i, j, k: (layer, k, j)),
            pl.BlockSpec((tm, tn), lambda i, j, k: (i, j)),
            pl.BlockSpec((1, tn), lambda i, j, k: (0, j)),
        ],
        out_specs=pl.BlockSpec((tm, tn), lambda i, j, k: (i, j)),
        out_shape=jax.ShapeDtypeStruct((m, n), F32),
        scratch_shapes=[pltpu.VMEM((tm, tn), F32)],
        compiler_params=_params("parallel", "parallel", "arbitrary"),
        name="peer_out",
    )(a, w, res, gate)


def _norm_kernel(x_ref, mul_ref, add_ref, o_ref):
    x = x_ref[...]
    ms = jnp.mean(x * x, axis=-1, keepdims=True)
    o_ref[...] = (x * lax.rsqrt(ms + NORM_EPS) * mul_ref[...] + add_ref[...]).astype(o_ref.dtype)


def _rms_mod(x, mul, add, out_dtype, tm=256):
    m, d = x.shape
    return pl.pallas_call(
        _norm_kernel,
        grid=(m // tm,),
        in_specs=[pl.BlockSpec((tm, d), lambda i: (i, 0)), pl.BlockSpec((1, d), lambda i: (0, 0)),
                  pl.BlockSpec((1, d), lambda i: (0, 0))],
        out_specs=pl.BlockSpec((tm, d), lambda i: (i, 0)),
        out_shape=jax.ShapeDtypeStruct((m, d), out_dtype),
        compiler_params=_params("parallel"),
        name="rms_mod",
    )(x, mul, add)


def _hg_lb_kernel(lg_ref, o_ref):
    lg = lg_ref[...]
    e = jnp.exp(lg - jnp.max(lg, axis=0, keepdims=True))
    p = e / jnp.sum(e, axis=0, keepdims=True)
    rows = [jnp.zeros_like(p[0:1])]
    for l in range(1, lg.shape[0]):
        rows.append(rows[-1] + p[l:l + 1])
    o_ref[...] = jnp.concatenate(rows, axis=0)


def _hg_lower_bounds(logits):
    return pl.pallas_call(_hg_lb_kernel, out_shape=jax.ShapeDtypeStruct(logits.shape, F32), name="hg_lb")(logits)


def _ret_body(q_ref, k_ref, v_ref, g_ref, cos_ref, sin_ref, dmat_ref, xi_ref, zeta_ref, gam_ref, o_ref, st_ref, first):
    c = RET_CHUNK
    dmat, xi, zeta, gam = dmat_ref[...], xi_ref[...], zeta_ref[...], gam_ref[...]
    st = jnp.where(first, 0.0, st_ref[...])
    for ci in range(q_ref.shape[0] // c):
        sl = slice(ci * c, (ci + 1) * c)
        q, k, v = q_ref[sl, :], k_ref[sl, :], v_ref[sl, :]
        cos, sin = cos_ref[sl, :], sin_ref[sl, :]
        half = RET_DK // 2
        qr = q * cos + pltpu.roll(q, half, 1) * sin
        kr = (k * cos + pltpu.roll(k, half, 1) * sin) * (RET_DK ** -0.5)
        scores = _dot(qr, kr, _NT) * dmat
        o = _dot(scores, v) + _dot(qr * xi, st)
        st = st * gam + _dot(kr * zeta, v, _TN)
        ms = jnp.mean(o * o, axis=-1, keepdims=True)
        g = g_ref[sl, :]
        y = o * lax.rsqrt(ms + NORM_EPS) * (g * jax.nn.sigmoid(g))
        o_ref[sl, :] = y.astype(o_ref.dtype)
    st_ref[...] = st


def _ret_kernel(*refs):
    _ret_body(*refs, pl.program_id(1) == 0)


def _retention(proj, tables, tb=512):
    t = proj.shape[0]
    h = RET_HEADS
    cos, sin, dmat, xi, zeta, gam = tables
    c = RET_CHUNK
    col = lambda g: pl.BlockSpec((tb, 128), lambda hh, n, g=g: (n, g * h + hh))
    tab = pl.BlockSpec((tb, 128), lambda hh, n: (n, 0))
    return pl.pallas_call(
        _ret_kernel,
        grid=(h, t // tb),
        in_specs=[col(0), col(1), col(2), col(3), tab, tab,
                  pl.BlockSpec((None, c, c), lambda hh, n: (hh, 0, 0)),
                  pl.BlockSpec((None, c, 128), lambda hh, n: (hh, 0, 0)),
                  pl.BlockSpec((None, c, 128), lambda hh, n: (hh, 0, 0)),
                  pl.BlockSpec((None, 1, 128), lambda hh, n: (hh, 0, 0))],
        out_specs=pl.BlockSpec((tb, 128), lambda hh, n: (n, hh)),
        out_shape=jax.ShapeDtypeStruct((t, h * 128), BF16),
        scratch_shapes=[pltpu.VMEM((RET_DK, 128), F32)],
        compiler_params=_params("parallel", "arbitrary"),
        name="retention",
    )(proj, proj, proj, proj, cos, sin, dmat, xi, zeta, gam)


def _retention_tables(t):
    half = RET_DK // 2
    inv = ROPE_BASE ** (-jnp.arange(half, dtype=F32) / half)
    ang = jnp.arange(t, dtype=F32)[:, None] * inv[None, :]
    cos, sin = jnp.cos(ang), jnp.sin(ang)
    cos2 = jnp.concatenate([cos, cos], axis=-1)
    sin2 = jnp.concatenate([-sin, sin], axis=-1)
    c = RET_CHUNK
    log_g = jnp.log(1.0 - 2.0 ** (-5.0 - jnp.arange(RET_HEADS, dtype=F32)))
    pos = jnp.arange(c, dtype=F32)
    rel = pos[:, None] - pos[None, :]
    dmat = jnp.where(rel >= 0, jnp.exp(log_g[:, None, None] * jnp.maximum(rel, 0.0)), 0.0)
    xi = jnp.broadcast_to(jnp.exp(log_g[:, None] * (pos + 1.0))[:, :, None], (RET_HEADS, c, 128))
    zeta = jnp.broadcast_to(jnp.exp(log_g[:, None] * (c - 1.0 - pos))[:, :, None], (RET_HEADS, c, 128))
    gam = jnp.broadcast_to(jnp.exp(log_g * c)[:, None, None], (RET_HEADS, 1, 128))
    return cos2, sin2, dmat, xi, zeta, gam


def _hg_body(q_ref, f_ref, v_ref, g_ref, lb_ref, o_ref, st_ref, first):
    n = HG_SUB
    tb = q_ref.shape[0]
    nb = tb // n
    lb = lb_ref[...]
    fg = lb + (1.0 - lb) * jax.nn.sigmoid(f_ref[...])
    kk = (1.0 - fg).reshape(nb, n, 128)
    cum = _dot_sel(_block_tril(tb, n), jnp.log(fg)).reshape(nb, n, 128)
    q, v = q_ref[...].reshape(nb, n, 128), v_ref[...].reshape(nb, n, 128)
    row = lax.broadcasted_iota(I32, (nb, n, 128), 1)
    o = jnp.zeros((nb, n, 128), F32)
    for s in range(n):
        d = jnp.where(row >= s, cum - cum[:, s:s + 1, :], -jnp.inf)
        col = jnp.sum(q * jnp.exp(d) * kk[:, s:s + 1, :], axis=-1, keepdims=True)
        o = o + col * v[:, s:s + 1, :]
    last = cum[:, n - 1:n, :]
    upd = lax.dot_general(v.astype(BF16), (kk * jnp.exp(last - cum)).astype(BF16), _BTN, preferred_element_type=F32)
    dec = jnp.exp(last)
    st = jnp.where(first, 0.0, st_ref[...])
    entering = []
    for j in range(nb):
        entering.append(st)
        st = st * dec[j] + upd[j]
    st_ref[...] = st
    o = o + lax.dot_general((q * jnp.exp(cum)).astype(BF16), jnp.stack(entering).astype(BF16), _BNT,
                            preferred_element_type=F32)
    o = o.reshape(tb, 128)
    ms = jnp.mean(o * o, axis=-1, keepdims=True)
    g = g_ref[...]
    o_ref[...] = (o * lax.rsqrt(ms + NORM_EPS) * (g * jax.nn.sigmoid(g))).astype(o_ref.dtype)


def _hg_kernel(*refs):
    _hg_body(*refs, pl.program_id(1) == 0)


def _hgrn2(proj, lb, tb=256):
    t = proj.shape[0]
    h = HG_HEADS
    col = lambda g: pl.BlockSpec((tb, 128), lambda hh, n, g=g: (n, (4 + g) * h + hh))
    return pl.pallas_call(
        _hg_kernel,
        grid=(h, t // tb),
        in_specs=[col(0), col(1), col(2), col(3), pl.BlockSpec((1, 128), lambda hh, n: (0, hh))],
        out_specs=pl.BlockSpec((tb, 128), lambda hh, n: (n, hh)),
        out_shape=jax.ShapeDtypeStruct((t, h * 128), BF16),
        scratch_shapes=[pltpu.VMEM((128, HG_DK), F32)],
        compiler_params=_params("parallel", "arbitrary"),
        name="hgrn2",
    )(proj, proj, proj, proj, lb)


def _rw_prep_values(cur_ref, prv_ref, mu_ref, w0_ref, w2_ref, a0_ref, a2_ref, g2_ref, vres_refs):
    cur = cur_ref[...]
    prev_row = jnp.where(pl.program_id(0) == 0, 0.0, prv_ref[7:8, :])
    row = lax.broadcasted_iota(I32, cur.shape, 0)
    shifted = jnp.where(row == 0, prev_row, pltpu.roll(cur, 1, 0))
    cols = cur + (shifted - cur) * mu_ref[...]
    w = RW_W
    r, k, v = cols[:, 0:w], cols[:, w:2 * w], cols[:, 2 * w:3 * w]
    wa = cols[:, 3 * w:3 * w + 128]
    lane = lax.broadcasted_iota(I32, wa.shape, 1)
    wa = jnp.where(lane < 64, jnp.tanh(wa), wa)
    gl = cols[:, 3 * w + 128:3 * w + 256]
    wlog = -jax.nn.softplus(-(w0_ref[...] + _dot(wa, w2_ref[...]))) - 0.5
    a = jax.nn.sigmoid(a0_ref[...] + _dot(wa, a2_ref[...]))
    if vres_refs is not None:
        vf_ref, v0_ref, v1_ref, v2_ref = vres_refs
        mix = jax.nn.sigmoid(v0_ref[...] + _dot(_dot(v, v1_ref[...]), v2_ref[...]))
        v = v + (vf_ref[...] - v) * mix
    ld = -jnp.exp(wlog)
    return r, ld, k, v, a, _dot(jax.nn.sigmoid(gl), g2_ref[...])


def _rw_level_masks(c):
    ti = lax.broadcasted_iota(I32, (c, c), 0)
    si = lax.broadcasted_iota(I32, (c, c), 1)
    masks = []
    n = 1
    while n < c:
        same = (ti // (2 * n)) == (si // (2 * n))
        masks.append(same & ((ti // n) % 2 == 1) & ((si // n) % 2 == 0))
        n *= 2
    return ti, si, masks


def _bdot(a, b, dims=_BNN):
    return lax.dot_general(a.astype(BF16), b.astype(BF16), dims, preferred_element_type=F32)


def _rw_kernel(*refs, tb, has_vres):
    n_prep = 12 if has_vres else 8
    vres_refs = refs[8:12] if has_vres else None
    kk_ref, ka_ref, rk_ref, lnw_ref, lnb_ref = refs[n_prep:n_prep + 5]
    outs = refs[n_prep + 5:]
    o_ref, st_ref = outs[0], outs[-1]
    r2d, ld2, k2d, v2d, a2d, g2d = _rw_prep_values(*refs[:8], vres_refs)
    if not has_vres:
        outs[1][...] = v2d
    c, nh, hd = RW_CHUNK, RW_HEADS, RW_HD
    nc = tb // c

    @pl.when(pl.program_id(0) == 0)
    def _():
        st_ref[...] = jnp.zeros_like(st_ref)

    def split(x):
        return jnp.stack([x[ci * c:(ci + 1) * c, h * hd:(h + 1) * hd] for ci in range(nc) for h in range(nh)])

    def split_row(ref):
        x = ref[...]
        return jnp.stack([x[:, h * hd:(h + 1) * hd] for _ in range(nc) for h in range(nh)])

    cw2 = _dot_sel(_block_tril(tb, c), ld2)
    r, ld, cw, k, v, a = split(r2d), split(ld2), split(cw2), split(k2d), split(v2d), split(a2d)
    kk_w, ka_w, rk_w, lnw, lnb = (split_row(p) for p in (kk_ref, ka_ref, rk_ref, lnw_ref, lnb_ref))
    ti, si, masks = _rw_level_masks(c)
    tril, strict = (ti >= si)[None], (ti > si)[None]

    wc = jnp.exp(cw)
    wprev = jnp.exp(cw - ld)
    winv = jnp.exp(-cw)
    wc_last = wc[:, c - 1:c, :]
    wtail = wc_last * winv
    kkf = k * kk_w
    kk = kkf / jnp.maximum(jnp.sqrt(jnp.sum(kkf * kkf, axis=-1, keepdims=True)), 1e-12)
    k2 = k * (1.0 + (a - 1.0) * ka_w)
    bv = kk * a
    at = -kk * wprev
    rt = r * wc
    ar = jnp.concatenate([at, rt], axis=1)
    gb = _bdot(ar, bv * winv, _BNT)
    gk = _bdot(ar, k2 * winv, _BNT)
    lab = jnp.where(strict, gb[:, :c], 0.0)
    lak = jnp.where(strict, gk[:, :c], 0.0)
    mrb = jnp.where(tril, gb[:, c:], 0.0)
    mrk = jnp.where(tril, gk[:, c:], 0.0)
    x = jnp.where((ti == si)[None], 1.0, 0.0) + jnp.where(masks[0][None], lab, 0.0)
    for m in masks[1:]:
        x = x + _bdot(_bdot(x, jnp.where(m[None], lab, 0.0)), x)
    lv = _bdot(jnp.concatenate([lak, mrk], axis=1), v)
    p = _bdot(x, at)
    q = _bdot(x, lv[:, :c])
    r2 = rt + _bdot(mrb, p)
    y0 = _bdot(mrb, q) + lv[:, c:]
    bh = bv * wtail
    g = _bdot(p, bh, _BTN)
    s_add = _bdot(q, bh, _BTN) + _bdot(v, k2 * wtail, _BTN)
    s = st_ref[...]
    entering = []
    for ci in range(nc):
        b0, b1 = ci * nh, (ci + 1) * nh
        entering.append(s)
        s = s * wc_last[b0:b1] + _bdot(s, g[b0:b1]) + s_add[b0:b1]
    st_ref[...] = s
    y = _bdot(r2, jnp.concatenate(entering, axis=0), _BNT) + y0
    mean = jnp.mean(y, axis=-1, keepdims=True)
    var = jnp.mean(jnp.square(y - mean), axis=-1, keepdims=True)
    yn = (y - mean) * lax.rsqrt(var + RW_LN_EPS) * lnw + lnb
    out = yn + jnp.sum(r * k2 * rk_w, axis=-1, keepdims=True) * v
    for ci in range(nc):
        rows = jnp.concatenate([out[ci * nh + h] for h in range(nh)], axis=1)
        o_ref[ci * c:(ci + 1) * c, :] = (rows * g2d[ci * c:(ci + 1) * c, :]).astype(o_ref.dtype)


def _rwkv7(cols, mu, w0, w2p, a0, a2p, g2, vres, k_k, k_a, r_k, ln_w, ln_b, tb=256):
    t, nc = cols.shape
    w = RW_W
    full = lambda shape: pl.BlockSpec(shape, lambda i: (0,) * len(shape))
    col = pl.BlockSpec((tb, w), lambda i: (i, 0))
    in_specs = [pl.BlockSpec((tb, nc), lambda i: (i, 0)),
                pl.BlockSpec((8, nc), lambda i: (jnp.maximum(i * (tb // 8) - 1, 0), 0)),
                full((1, nc)), full((1, w)), full((128, w)), full((1, w)), full((128, w)), full((128, w))]
    args = [cols, cols, mu, w0, w2p, a0, a2p, g2]
    if vres is not None:
        v_first, v0, v1p, v2p = vres
        in_specs += [col, full((1, w)), full((w, 128)), full((128, w))]
        args += [v_first, v0, v1p, v2p]
    in_specs += [full((1, w))] * 5
    args += [k_k, k_a, r_k, ln_w, ln_b]
    out_specs, out_shape = [col], [jax.ShapeDtypeStruct((t, w), BF16)]
    if vres is None:
        out_specs.append(col)
        out_shape.append(jax.ShapeDtypeStruct((t, w), F32))
    res = pl.pallas_call(
        functools.partial(_rw_kernel, tb=tb, has_vres=vres is not None),
        grid=(t // tb,),
        in_specs=in_specs,
        out_specs=out_specs,
        out_shape=out_shape,
        scratch_shapes=[pltpu.VMEM((RW_HEADS, RW_HD, RW_HD), F32)],
        compiler_params=_params("arbitrary"),
        name="rwkv7",
    )(*args)
    return (res[0], res[1]) if vres is None else (res[0], None)


def _fox_cum_kernel(h_ref, wt_ref, w_ref, fbc_ref, fbr_ref, row_ref, col_ref, crow_ref, ccol_ref):
    @pl.when(pl.program_id(0) == 0)
    def _():
        crow_ref[...] = jnp.zeros_like(crow_ref)
        ccol_ref[...] = jnp.zeros_like(ccol_ref)

    h = h_ref[...]
    tt = h.shape[0]
    ti = lax.broadcasted_iota(I32, (tt, tt), 0)
    si = lax.broadcasted_iota(I32, (tt, tt), 1)
    lf_r = jax.nn.log_sigmoid(_dot(wt_ref[...], h, _NT) + fbc_ref[...])
    cum_r = _dot_sel((ti <= si).astype(F32), lf_r, x_first=True) + crow_ref[:, 0:1]
    row_ref[...] = cum_r
    crow_ref[...] = jnp.broadcast_to(cum_r[:, tt - 1:tt], crow_ref.shape)
    lf_c = jax.nn.log_sigmoid(_dot(h, w_ref[...]) + fbr_ref[...])
    cum_c = _dot_sel((ti >= si).astype(F32), lf_c) + ccol_ref[0:1, :]
    col_ref[...] = cum_c
    ccol_ref[...] = jnp.broadcast_to(cum_c[tt - 1:tt, :], ccol_ref.shape)


def _fox_cum(h, w_ff_t, fb, tt=512):
    t, d = h.shape
    nh = FOX_HEADS
    full = lambda shape: pl.BlockSpec(shape, lambda i: (0,) * len(shape))
    return pl.pallas_call(
        _fox_cum_kernel,
        grid=(t // tt,),
        in_specs=[pl.BlockSpec((tt, d), lambda i: (i, 0)), full((nh, d)), full((d, nh)), full((nh, 1)), full((1, nh))],
        out_specs=[pl.BlockSpec((nh, tt), lambda i: (0, i)), pl.BlockSpec((tt, nh), lambda i: (i, 0))],
        out_shape=[jax.ShapeDtypeStruct((nh, t), F32), jax.ShapeDtypeStruct((t, nh), F32)],
        scratch_shapes=[pltpu.VMEM((nh, 128), F32), pltpu.VMEM((8, nh), F32)],
        compiler_params=_params("arbitrary"),
        name="fox_cum",
    )(h, w_ff_t, w_ff_t.T, fb.reshape(nh, 1), fb.reshape(1, nh))


def _fox_kernel(q_ref, k_ref, v_ref, cq_ref, ck_ref, o_ref, m_ref, l_ref, acc_ref, *, tq):
    hh, qi = pl.program_id(0), pl.program_id(1)
    nk = k_ref.shape[0] // tq
    q = q_ref[...]
    lane = lax.broadcasted_iota(I32, cq_ref.shape, 1)
    cq = jnp.sum(jnp.where(lane == hh, cq_ref[...], 0.0), axis=-1, keepdims=True)
    m_ref[...] = jnp.full_like(m_ref, -jnp.inf)
    l_ref[...] = jnp.zeros_like(l_ref)
    acc_ref[...] = jnp.zeros_like(acc_ref)

    log2e = 1.4426950408889634
    cq2 = cq * log2e

    def block(ki, diagonal):
        rows = pl.ds(pl.multiple_of(ki * tq, tq), tq)
        s = _dot(q, k_ref[rows, :], _NT) * (FOX_HD ** -0.5 * log2e) - ck_ref[pl.ds(hh * nk + ki, 1), :] * log2e
        if diagonal:
            s = jnp.where(lax.broadcasted_iota(I32, s.shape, 1) <= lax.broadcasted_iota(I32, s.shape, 0), s, -jnp.inf)
        m_old = m_ref[...]
        m_new = jnp.maximum(m_old, jnp.max(s, axis=-1, keepdims=True) + cq2)
        alpha = jnp.exp2(m_old - m_new)
        p = jnp.exp2(s - (m_new - cq2))
        l_ref[...] = alpha * l_ref[...] + jnp.sum(p, axis=-1, keepdims=True)
        acc_ref[...] = alpha * acc_ref[...] + _dot(p, v_ref[rows, :])
        m_ref[...] = m_new

    def body(ki, carry):
        block(ki, False)
        return carry

    lax.fori_loop(0, qi, body, 0)
    block(qi, True)
    o_ref[...] = (acc_ref[...] / l_ref[...]).astype(o_ref.dtype)


def _fox_attention(qkv, cum_row, cum_col, tq=2048):
    t = qkv.shape[0]
    h = FOX_HEADS
    nq = t // tq
    return pl.pallas_call(
        functools.partial(_fox_kernel, tq=tq),
        grid=(h, nq),
        in_specs=[pl.BlockSpec((tq, 128), lambda hh, qi: (qi, hh)),
                  pl.BlockSpec((t, 128), lambda hh, qi: (0, h + hh)),
                  pl.BlockSpec((t, 128), lambda hh, qi: (0, 2 * h + hh)),
                  pl.BlockSpec((tq, h), lambda hh, qi: (qi, 0)),
                  pl.BlockSpec((h * nq, tq), lambda hh, qi: (0, 0))],
        out_specs=pl.BlockSpec((tq, 128), lambda hh, qi: (qi, hh)),
        out_shape=jax.ShapeDtypeStruct((t, h * 128), BF16),
        scratch_shapes=[pltpu.VMEM((tq, 1), F32), pltpu.VMEM((tq, 1), F32), pltpu.VMEM((tq, 128), F32)],
        compiler_params=_params("parallel", "arbitrary"),
        name="fox_attention",
    )(qkv, qkv, qkv, cum_col, cum_row.reshape(h * nq, tq))


def _merge_kernel(*refs):
    y_refs, w_refs, g_refs, o_ref = refs[0:4], refs[4:8], refs[8:12], refs[12]
    acc = None
    for n in range(N_BRANCH):
        term = jax.nn.sigmoid(g_refs[n][...]) * _dot(y_refs[n][...], w_refs[n][...])
        acc = term if acc is None else acc + term
    o_ref[...] = acc.astype(o_ref.dtype)


def _merge(branches, w_branch, layer, gates, tm=1024, tn=512):
    t = branches[0].shape[0]
    d = D_MODEL
    nb = d // tn
    y_spec = pl.BlockSpec((tm, BRANCH_W), lambda i, j: (i, 0))
    w_specs = [pl.BlockSpec((None, None, BRANCH_W, tn), lambda i, j, n=n: (layer, n, 0, j)) for n in range(N_BRANCH)]
    g_specs = [pl.BlockSpec((tm, tn), lambda i, j, n=n: (i, n * nb + j)) for n in range(N_BRANCH)]
    return pl.pallas_call(
        _merge_kernel,
        grid=(t // tm, nb),
        in_specs=[y_spec] * 4 + w_specs + g_specs,
        out_specs=pl.BlockSpec((tm, tn), lambda i, j: (i, j)),
        out_shape=jax.ShapeDtypeStruct((t, d), BF16),
        compiler_params=_params("parallel", "parallel"),
        name="merge",
    )(*branches, *([w_branch] * N_BRANCH), *([gates] * N_BRANCH))


def _topk_rows(s, k, payloads=()):
    n = s.shape[0]
    row = lax.broadcasted_iota(I32, s.shape, 0).astype(F32)
    vals, rows, picked = [], [], [[] for _ in payloads]
    for _ in range(k):
        m = jnp.max(s, axis=0, keepdims=True)
        first = jnp.min(jnp.where(s == m, row, float(n)), axis=0, keepdims=True)
        sel = row == first
        vals.append(m)
        rows.append(first)
        for out, p in zip(picked, payloads):
            out.append(jnp.max(jnp.where(sel, p, -1.0), axis=0, keepdims=True))
        s = jnp.where(sel, -jnp.inf, s)
    cat = lambda xs: jnp.concatenate(xs, axis=0)
    return cat(vals), cat(rows), [cat(p) for p in picked]


_PEER_PAIRS = [(r1, r2) for r1 in range(PEER_TOPK) for r2 in range(PEER_TOPK) if (r1 + 1) * (r2 + 1) <= PEER_TOPK]


def _peer_topk_kernel(q_ref, keys_ref, a_ref, b_ref, g_ref):
    kq = PEER_TOPK
    half = PEER_DKEY // 2
    a_rows, b_rows, g_rows = [], [], []
    for hh in range(PEER_HEADS):
        top = []
        for p in range(2):
            qs = q_ref[:, (2 * hh + p) * half:(2 * hh + p + 1) * half]
            s = _dot(keys_ref[hh, p], qs, _NT)
            top.append(_topk_rows(s, kq)[:2])
        (v1, i1), (v2, i2) = top
        cand = jnp.concatenate([v1[r1:r1 + 1, :] + v2[r2:r2 + 1, :] for r1, r2 in _PEER_PAIRS], axis=0)
        key1 = jnp.concatenate([i1[r1:r1 + 1, :] for r1, _ in _PEER_PAIRS], axis=0)
        key2 = jnp.concatenate([i2[r2:r2 + 1, :] for _, r2 in _PEER_PAIRS], axis=0)
        best, _, (e1, e2) = _topk_rows(cand, kq, (key1, key2))
        ex = jnp.exp(best - best[0:1, :])
        a_rows.append(e1)
        b_rows.append(e2)
        g_rows.append(ex / jnp.sum(ex, axis=0, keepdims=True))
    a_ref[...] = jnp.concatenate(a_rows, axis=0).T.astype(I32)
    b_ref[...] = jnp.concatenate(b_rows, axis=0).T.astype(I32)
    g_ref[...] = jnp.concatenate(g_rows, axis=0).T


def _peer_topk(q, keys, tt=512):
    t = q.shape[0]
    n = PEER_HEADS * PEER_TOPK
    spec = pl.BlockSpec((tt, n), lambda i: (i, 0))
    return pl.pallas_call(
        _peer_topk_kernel,
        grid=(t // tt,),
        in_specs=[pl.BlockSpec((tt, q.shape[1]), lambda i: (i, 0)),
                  pl.BlockSpec(keys.shape, lambda i: (0, 0, 0, 0))],
        out_specs=[spec, spec, spec],
        out_shape=[jax.ShapeDtypeStruct((t, n), I32), jax.ShapeDtypeStruct((t, n), I32), jax.ShapeDtypeStruct((t, n), F32)],
        compiler_params=_params("parallel"),
        name="peer_topk",
    )(q, keys)


def _peer_weights_kernel(a_ref, b_ref, g_ref, o_ref):
    tt = a_ref.shape[0]
    nk = PEER_NKEYS
    sub = lax.broadcasted_iota(I32, (nk, a_ref.shape[1]), 0)

    group = 64

    def body(i, carry):
        t0 = pl.multiple_of(i * group, group)
        lefts, rights = [], []
        for u in range(group):
            a = a_ref[pl.ds(t0 + u, 1), :]
            b = b_ref[pl.ds(t0 + u, 1), :]
            g = g_ref[pl.ds(t0 + u, 1), :]
            g_hi = g.astype(BF16).astype(F32)
            g_lo = g - g_hi
            lefts.append(jnp.concatenate([jnp.where(sub == a, g_hi, 0.0), jnp.where(sub == a, g_lo, 0.0)], axis=1))
            onehot = jnp.where(sub == b, 1.0, 0.0)
            rights.append(jnp.concatenate([onehot, onehot], axis=1))
        o_ref[pl.ds(t0, group)] = _bdot(jnp.stack(lefts), jnp.stack(rights), _BNT)
        return carry

    lax.fori_loop(0, tt // group, body, 0)


def _peer_weights(a, b, g, tt=128):
    t, n = a.shape
    spec = pl.BlockSpec((tt, n), lambda i: (i, 0))
    return pl.pallas_call(
        _peer_weights_kernel,
        grid=(t // tt,),
        in_specs=[spec, spec, spec],
        out_specs=pl.BlockSpec((tt, PEER_NKEYS, PEER_NKEYS), lambda i: (i, 0, 0)),
        out_shape=jax.ShapeDtypeStruct((t, PEER_NKEYS, PEER_NKEYS), F32),
        compiler_params=_params("parallel"),
        name="peer_weights",
    )(a, b, g)


def _peer_act_kernel(h_ref, u_ref, w_ref, o_ref, *, ne1):
    tm = h_ref.shape[0]
    z = _dot(h_ref[...], u_ref[...], _NT)
    gelu = 0.5 * z * (1.0 + lax.erf(z * (2.0 ** -0.5)))
    w2 = w_ref.reshape(tm * ne1, PEER_NKEYS)
    w = jnp.concatenate([w2[pl.ds(j, tm, stride=ne1), :] for j in range(ne1)], axis=1)
    o_ref[...] = (gelu * w).astype(o_ref.dtype)


def _peer_act(h, u, layer, wts, tm=1024, te=1024):
    t, d = h.shape
    e = u.shape[1]
    ne1 = te // PEER_NKEYS
    return pl.pallas_call(
        functools.partial(_peer_act_kernel, ne1=ne1),
        grid=(t // tm, e // te),
        in_specs=[pl.BlockSpec((tm, d), lambda i, j: (i, 0), pipeline_mode=pl.Buffered(1)),
                  pl.BlockSpec((None, te, d), lambda i, j: (layer, j, 0)),
                  pl.BlockSpec((tm, ne1, PEER_NKEYS), lambda i, j: (i, j, 0))],
        out_specs=pl.BlockSpec((tm, te), lambda i, j: (i, j)),
        out_shape=jax.ShapeDtypeStruct((t, e), BF16),
        compiler_params=_params("parallel", "parallel"),
        name="peer_act",
    )(h, u, wts)


def kernel(x, c, w_in, w_branch, w_out, ada_w, ada_b, ada_table, hg_lb_logits, rw_mu, rw_w0, rw_w2, rw_a0, rw_a2, rw_g2, rw_k_k, rw_k_a, rw_r_k, rw_ln_w, rw_ln_b, rw_v0, rw_v1, rw_v2, fox_fb, peer_wq, peer_keys, peer_u, peer_v, final_norm_w):
    b, t, d = x.shape
    assert b == 1 and d == D_MODEL
    assert w_in.shape[2] == COL_GATE + N_BRANCH * d and COL_RW % 256 == 0 and COL_FOX % 256 == 0 and COL_FF % 256 == 0
    depth = w_in.shape[0]
    x = x.reshape(t, d)

    c8 = jnp.broadcast_to(c, (8, d))
    mods = _matmul(c8, ada_w, name="ada_mods", n_out=N_MOD * d, tm=8, tn=512, out_dtype=F32, pre="silu",
                   bias=ada_b.reshape(1, -1))[0].reshape(N_MOD, d)
    lbs = _hg_lower_bounds(hg_lb_logits)
    ret_tables = _retention_tables(t)
    w_in_t = jnp.swapaxes(w_in, 1, 2)
    w_ff_t = w_in_t[:, COL_FF:COL_GATE, :]
    u_bf, v_bf = peer_u.astype(BF16), peer_v.astype(BF16)
    zpad = lambda z, axis, n: jnp.pad(z, [(0, n - z.shape[i]) if i == axis else (0, 0) for i in range(z.ndim)])

    v_first = None
    for l in range(depth):
        m = mods + ada_table[l]
        sh1, sc1, g1, sh2, sc2, g2 = [m[i:i + 1] for i in range(N_MOD)]
        h = _rms_mod(x, 1.0 + sc1, sh1, BF16)
        p_rh = _matmul(h, w_in_t, name="proj_ret_hg", layer=l, wt_row0=0, n_out=COL_RW, tm=MM_TM, tn=MM_TN, out_dtype=F32)
        p_rw = _matmul(h, w_in_t, name="proj_rwkv", layer=l, wt_row0=COL_RW, n_out=RW_COLS, tm=MM_TM, tn=MM_TN // 2,
                       out_dtype=F32)
        p_fox = _matmul(h, w_in_t, name="proj_fox", layer=l, wt_row0=COL_FOX, n_out=COL_FF - COL_FOX, tm=MM_TM // 2,
                        tn=MM_TN, out_dtype=BF16)
        p_gate = _matmul(h, w_in_t, name="proj_gate", layer=l, wt_row0=COL_GATE, n_out=N_BRANCH * d, tm=MM_TM, tn=MM_TN,
                         out_dtype=F32)
        ya = _retention(p_rh, ret_tables)
        yb = _hgrn2(p_rh, lbs[l:l + 1])
        w2p = jnp.concatenate([rw_w2[l], jnp.zeros_like(rw_a2[l])], axis=0)
        a2p = jnp.concatenate([jnp.zeros_like(rw_w2[l]), rw_a2[l]], axis=0)
        vres = None
        if l > 0:
            vres = (v_first, rw_v0[l - 1][None], zpad(rw_v1[l - 1], 1, 128), zpad(rw_v2[l - 1], 0, 128))
        yc, v_own = _rwkv7(p_rw, rw_mu[l][None], rw_w0[l][None], w2p, rw_a0[l][None], a2p, rw_g2[l], vres, rw_k_k[l][None],
                           rw_k_a[l][None], rw_r_k[l].reshape(1, RW_W), rw_ln_w[l][None], rw_ln_b[l][None])
        if l == 0:
            v_first = v_own
        cum_row, cum_col = _fox_cum(h, w_ff_t[l], fox_fb[l])
        yd = _fox_attention(p_fox, cum_row, cum_col)
        merged = _merge([ya, yb, yc, yd], w_branch, l, p_gate)
        x = _matmul(merged, w_out, name="out_proj", layer=l, n_out=d, tm=MM_TM, tn=MM_TN, out_dtype=F32, res=x, gate=g1)
        h2 = _rms_mod(x, 1.0 + sc2, sh2, BF16)
        q = _matmul(h2, peer_wq, name="peer_query", layer=l, n_out=PEER_HEADS * PEER_DKEY, tm=MM_TM // 2, tn=MM_TN,
                    out_dtype=F32)
        ea, eb, eg = _peer_topk(q, peer_keys[l])
        wts = _peer_weights(ea, eb, eg)
        act = _peer_act(h2, u_bf, l, wts)
        x = _matmul_acc_res(act, v_bf, l, x, g2, tm=1024, tn=1024, tk=2048)
    out = _rms_mod(x, final_norm_w[None], jnp.zeros((1, d), F32), F32)
    return out.reshape(b, t, d)
```

```python
import functools

import jax
import jax.numpy as jnp
import numpy as np
from jax import lax
from jax.experimental import pallas as pl
from jax.experimental.pallas import tpu as pltpu

F32, BF16, I32 = jnp.float32, jnp.bfloat16, jnp.int32

D_MODEL = 4096
NORM_EPS = 1e-6
N_MOD = 6
RET_HEADS, RET_DK, RET_CHUNK, ROPE_BASE = 8, 128, 128, 10000.0
HG_HEADS, HG_DK = 8, 128
HG_SUB = 16
RW_HEADS, RW_HD = 16, 64
RW_W = RW_HEADS * RW_HD
RW_CHUNK = 64
RW_LN_EPS = 64e-5
FOX_HEADS, FOX_HD = 8, 128
N_BRANCH, BRANCH_W = 4, 1024
PEER_HEADS, PEER_NKEYS, PEER_DKEY, PEER_TOPK = 8, 128, 256, 16
PEER_EXPERTS = PEER_NKEYS * PEER_NKEYS
COL_RW = 8192
RW_COLS = 3 * RW_W + 64 + 64 + 128
COL_FOX = COL_RW + RW_COLS
COL_FF = COL_FOX + 3 * FOX_HEADS * FOX_HD
COL_GATE = COL_FF + FOX_HEADS

V7X_VMEM_LIMIT_BYTES = 56 * 1024 * 1024
MM_TM, MM_TN = 2048, 512

_NN = (((1,), (0,)), ((), ()))
_NT = (((1,), (1,)), ((), ()))
_TN = (((0,), (0,)), ((), ()))


def _dot(a, b, dims=_NN):
    return lax.dot_general(a.astype(BF16), b.astype(BF16), dims, preferred_element_type=F32)


_BNN = (((2,), (1,)), ((0,), (0,)))
_BNT = (((2,), (2,)), ((0,), (0,)))
_BTN = (((1,), (1,)), ((0,), (0,)))


def _block_tril(size, block):
    ti = lax.broadcasted_iota(I32, (size, size), 0)
    si = lax.broadcasted_iota(I32, (size, size), 1)
    return ((ti >= si) & (ti // block == si // block)).astype(F32)


def _dot_sel(sel, x, x_first=False):
    hi = x.astype(BF16)
    r1 = x - hi.astype(F32)
    mid = r1.astype(BF16)
    lo = (r1 - mid.astype(F32)).astype(BF16)
    s = sel.astype(BF16)
    out = None
    for part in (hi, mid, lo):
        ab = (part, s) if x_first else (s, part)
        term = lax.dot_general(*ab, _NN, preferred_element_type=F32)
        out = term if out is None else out + term
    return out


def _params(*sem):
    return pltpu.CompilerParams(dimension_semantics=sem, vmem_limit_bytes=V7X_VMEM_LIMIT_BYTES)


def _mm_kernel(*refs, pre, has_bias, has_res, w_transposed):
    a_ref, w_ref = refs[0], refs[1]
    k = 2
    a = a_ref[...]
    if pre == "silu":
        a = a * jax.nn.sigmoid(a)
    y = _dot(a, w_ref[0], _NT) if w_transposed else _dot(a, w_ref[...])
    if has_bias:
        y = y + refs[k][...]
        k += 1
    if has_res:
        y = refs[k][...] + refs[k + 1][...] * y
        k += 2
    refs[k][...] = y.astype(refs[k].dtype)


def _matmul(a, w, *, name, n_out, tm, tn, out_dtype, layer=None, col_block0=0, wt_row0=None, pre=None, bias=None, res=None,
            gate=None):
    m, kdim = a.shape
    assert m % tm == 0 and (wt_row0 is None or (n_out % tn == 0 and wt_row0 % 8 == 0))
    grid = (m // tm, pl.cdiv(n_out, tn))
    if wt_row0 is not None:
        w_spec = pl.BlockSpec((pl.Element(1), pl.Element(tn), pl.Element(kdim)),
                              lambda i, j: (layer, pl.multiple_of(wt_row0 + j * tn, 8), 0))
    elif layer is None:
        w_spec = pl.BlockSpec((kdim, tn), lambda i, j: (0, j + col_block0))
    else:
        w_spec = pl.BlockSpec((None, kdim, tn), lambda i, j: (layer, 0, j + col_block0))
    a_mode = pl.Buffered(1) if grid[1] >= 8 else None
    in_specs = [pl.BlockSpec((tm, kdim), lambda i, j: (i, 0), pipeline_mode=a_mode), w_spec]
    args = [a, w]
    if bias is not None:
        in_specs.append(pl.BlockSpec((1, tn), lambda i, j: (0, j)))
        args.append(bias)
    if res is not None:
        in_specs += [pl.BlockSpec((tm, tn), lambda i, j: (i, j)), pl.BlockSpec((1, tn), lambda i, j: (0, j))]
        args += [res, gate]
    return pl.pallas_call(
        functools.partial(_mm_kernel, pre=pre, has_bias=bias is not None, has_res=res is not None,
                          w_transposed=wt_row0 is not None),
        grid=grid,
        in_specs=in_specs,
        out_specs=pl.BlockSpec((tm, tn), lambda i, j: (i, j)),
        out_shape=jax.ShapeDtypeStruct((m, n_out), out_dtype),
        compiler_params=_params("parallel", "parallel"),
        name=name,
    )(*args)


def _mm_acc_kernel(a_ref, w_ref, x_ref, g_ref, o_ref, acc_ref):
    @pl.when(pl.program_id(2) == 0)
    def _():
        acc_ref[...] = jnp.zeros_like(acc_ref)

    acc_ref[...] += _dot(a_ref[...], w_ref[...])

    @pl.when(pl.program_id(2) == pl.num_programs(2) - 1)
    def _():
        o_ref[...] = x_ref[...] + g_ref[...] * acc_ref[...]


def _matmul_acc_res(a, w, layer, res, gate, *, tm, tn, tk):
    m, kdim = a.shape
    n = w.shape[2]
    assert kdim % tk == 0 and m % tm == 0 and n % tn == 0
    return pl.pallas_call(
        _mm_acc_kernel,
        grid=(m // tm, n // tn, kdim // tk),
        in_specs=[
            pl.BlockSpec((tm, tk), lambda i, j, k: (i, k)),
            pl.BlockSpec((None, tk, tn), lambda i, j, k: (layer, k, j)),
            pl.BlockSpec((tm, tn), lambda i, j, k: (i, j)),
            pl.BlockSpec((1, tn), lambda i, j, k: (0, j)),
        ],
        out_specs=pl.BlockSpec((tm, tn), lambda i, j, k: (i, j)),
        out_shape=jax.ShapeDtypeStruct((m, n), F32),
        scratch_shapes=[pltpu.VMEM((tm, tn), F32)],
        compiler_params=_params("parallel", "parallel", "arbitrary"),
        name="peer_out",
    )(a, w, res, gate)


def _norm_kernel(x_ref, mul_ref, add_ref, o_ref):
    x = x_ref[...]
    ms = jnp.mean(x * x, axis=-1, keepdims=True)
    o_ref[...] = (x * lax.rsqrt(ms + NORM_EPS) * mul_ref[...] + add_ref[...]).astype(o_ref.dtype)


def _rms_mod(x, mul, add, out_dtype, tm=256):
    m, d = x.shape
    return pl.pallas_call(
        _norm_kernel,
        grid=(m // tm,),
        in_specs=[pl.BlockSpec((tm, d), lambda i: (i, 0)), pl.BlockSpec((1, d), lambda i: (0, 0)),
                  pl.BlockSpec((1, d), lambda i: (0, 0))],
        out_specs=pl.BlockSpec((tm, d), lambda i: (i, 0)),
        out_shape=jax.ShapeDtypeStruct((m, d), out_dtype),
        compiler_params=_params("parallel"),
        name="rms_mod",
    )(x, mul, add)


def _hg_lb_kernel(lg_ref, o_ref):
    lg = lg_ref[...]
    e = jnp.exp(lg - jnp.max(lg, axis=0, keepdims=True))
    p = e / jnp.sum(e, axis=0, keepdims=True)
    rows = [jnp.zeros_like(p[0:1])]
    for l in range(1, lg.shape[0]):
        rows.append(rows[-1] + p[l:l + 1])
    o_ref[...] = jnp.concatenate(rows, axis=0)


def _hg_lower_bounds(logits):
    return pl.pallas_call(_hg_lb_kernel, out_shape=jax.ShapeDtypeStruct(logits.shape, F32), name="hg_lb")(logits)


def _ret_body(q_ref, k_ref, v_ref, g_ref, cos_ref, sin_ref, dmat_ref, xi_ref, zeta_ref, gam_ref, o_ref, st_ref, first):
    c = RET_CHUNK
    dmat, xi, zeta, gam = dmat_ref[...], xi_ref[...], zeta_ref[...], gam_ref[...]
    st = jnp.where(first, 0.0, st_ref[...])
    for ci in range(q_ref.shape[0] // c):
        sl = slice(ci * c, (ci + 1) * c)
        q, k, v = q_ref[sl, :], k_ref[sl, :], v_ref[sl, :]
        cos, sin = cos_ref[sl, :], sin_ref[sl, :]
        half = RET_DK // 2
        qr = q * cos + pltpu.roll(q, half, 1) * sin
        kr = (k * cos + pltpu.roll(k, half, 1) * sin) * (RET_DK ** -0.5)
        scores = _dot(qr, kr, _NT) * dmat
        o = _dot(scores, v) + _dot(qr * xi, st)
        st = st * gam + _dot(kr * zeta, v, _TN)
        ms = jnp.mean(o * o, axis=-1, keepdims=True)
        g = g_ref[sl, :]
        y = o * lax.rsqrt(ms + NORM_EPS) * (g * jax.nn.sigmoid(g))
        o_ref[sl, :] = y.astype(o_ref.dtype)
    st_ref[...] = st


def _ret_kernel(*refs):
    _ret_body(*refs, pl.program_id(1) == 0)


def _retention(proj, tables, tb=2048):
    t = proj.shape[0]
    h = RET_HEADS
    cos, sin, dmat, xi, zeta, gam = tables
    c = RET_CHUNK
    col = lambda g: pl.BlockSpec((tb, 128), lambda hh, n, g=g: (n, g * h + hh))
    tab = pl.BlockSpec((tb, 128), lambda hh, n: (n, 0))
    return pl.pallas_call(
        _ret_kernel,
        grid=(h, t // tb),
        in_specs=[col(0), col(1), col(2), col(3), tab, tab,
                  pl.BlockSpec((None, c, c), lambda hh, n: (hh, 0, 0)),
                  pl.BlockSpec((None, c, 128), lambda hh, n: (hh, 0, 0)),
                  pl.BlockSpec((None, c, 128), lambda hh, n: (hh, 0, 0)),
                  pl.BlockSpec((None, 1, 128), lambda hh, n: (hh, 0, 0))],
        out_specs=pl.BlockSpec((tb, 128), lambda hh, n: (n, hh)),
        out_shape=jax.ShapeDtypeStruct((t, h * 128), BF16),
        scratch_shapes=[pltpu.VMEM((RET_DK, 128), F32)],
        compiler_params=_params("parallel", "arbitrary"),
        name="retention",
    )(proj, proj, proj, proj, cos, sin, dmat, xi, zeta, gam)


def _retention_tables(t):
    half = RET_DK // 2
    inv = ROPE_BASE ** (-jnp.arange(half, dtype=F32) / half)
    ang = jnp.arange(t, dtype=F32)[:, None] * inv[None, :]
    cos, sin = jnp.cos(ang), jnp.sin(ang)
    cos2 = jnp.concatenate([cos, cos], axis=-1)
    sin2 = jnp.concatenate([-sin, sin], axis=-1)
    c = RET_CHUNK
    log_g = jnp.log(1.0 - 2.0 ** (-5.0 - jnp.arange(RET_HEADS, dtype=F32)))
    pos = jnp.arange(c, dtype=F32)
    rel = pos[:, None] - pos[None, :]
    dmat = jnp.where(rel >= 0, jnp.exp(log_g[:, None, None] * jnp.maximum(rel, 0.0)), 0.0)
    xi = jnp.broadcast_to(jnp.exp(log_g[:, None] * (pos + 1.0))[:, :, None], (RET_HEADS, c, 128))
    zeta = jnp.broadcast_to(jnp.exp(log_g[:, None] * (c - 1.0 - pos))[:, :, None], (RET_HEADS, c, 128))
    gam = jnp.broadcast_to(jnp.exp(log_g * c)[:, None, None], (RET_HEADS, 1, 128))
    return cos2, sin2, dmat, xi, zeta, gam


def _hg_body(q_ref, f_ref, v_ref, g_ref, lb_ref, o_ref, st_ref, first):
    n = HG_SUB
    tb = q_ref.shape[0]
    nb = tb // n
    lb = lb_ref[...]
    fg = lb + (1.0 - lb) * jax.nn.sigmoid(f_ref[...])
    kk = (1.0 - fg).reshape(nb, n, 128)
    cum = _dot_sel(_block_tril(tb, n), jnp.log(fg)).reshape(nb, n, 128)
    q, v = q_ref[...].reshape(nb, n, 128), v_ref[...].reshape(nb, n, 128)
    row = lax.broadcasted_iota(I32, (nb, n, 128), 1)
    o = jnp.zeros((nb, n, 128), F32)
    for s in range(n):
        d = jnp.where(row >= s, cum - cum[:, s:s + 1, :], -jnp.inf)
        col = jnp.sum(q * jnp.exp(d) * kk[:, s:s + 1, :], axis=-1, keepdims=True)
        o = o + col * v[:, s:s + 1, :]
    last = cum[:, n - 1:n, :]
    upd = lax.dot_general(v.astype(BF16), (kk * jnp.exp(last - cum)).astype(BF16), _BTN, preferred_element_type=F32)
    dec = jnp.exp(last)
    st = jnp.where(first, 0.0, st_ref[...])
    entering = []
    for j in range(nb):
        entering.append(st)
        st = st * dec[j] + upd[j]
    st_ref[...] = st
    o = o + lax.dot_general((q * jnp.exp(cum)).astype(BF16), jnp.stack(entering).astype(BF16), _BNT,
                            preferred_element_type=F32)
    o = o.reshape(tb, 128)
    ms = jnp.mean(o * o, axis=-1, keepdims=True)
    g = g_ref[...]
    o_ref[...] = (o * lax.rsqrt(ms + NORM_EPS) * (g * jax.nn.sigmoid(g))).astype(o_ref.dtype)


def _hg_kernel(*refs):
    _hg_body(*refs, pl.program_id(1) == 0)


def _hgrn2(proj, lb, tb=256):
    t = proj.shape[0]
    h = HG_HEADS
    col = lambda g: pl.BlockSpec((tb, 128), lambda hh, n, g=g: (n, (4 + g) * h + hh))
    return pl.pallas_call(
        _hg_kernel,
        grid=(h, t // tb),
        in_specs=[col(0), col(1), col(2), col(3), pl.BlockSpec((1, 128), lambda hh, n: (0, hh))],
        out_specs=pl.BlockSpec((tb, 128), lambda hh, n: (n, hh)),
        out_shape=jax.ShapeDtypeStruct((t, h * 128), BF16),
        scratch_shapes=[pltpu.VMEM((128, HG_DK), F32)],
        compiler_params=_params("parallel", "arbitrary"),
        name="hgrn2",
    )(proj, proj, proj, proj, lb)


def _rw_prep_values(cur_ref, prv_ref, mu_ref, w0_ref, w2_ref, a0_ref, a2_ref, g2_ref, vres_refs):
    cur = cur_ref[...]
    prev_row = jnp.where(pl.program_id(0) == 0, 0.0, prv_ref[7:8, :])
    row = lax.broadcasted_iota(I32, cur.shape, 0)
    shifted = jnp.where(row == 0, prev_row, pltpu.roll(cur, 1, 0))
    cols = cur + (shifted - cur) * mu_ref[...]
    w = RW_W
    r, k, v = cols[:, 0:w], cols[:, w:2 * w], cols[:, 2 * w:3 * w]
    wa = cols[:, 3 * w:3 * w + 128]
    lane = lax.broadcasted_iota(I32, wa.shape, 1)
    wa = jnp.where(lane < 64, jnp.tanh(wa), wa)
    gl = cols[:, 3 * w + 128:3 * w + 256]
    wlog = -jax.nn.softplus(-(w0_ref[...] + _dot(wa, w2_ref[...]))) - 0.5
    a = jax.nn.sigmoid(a0_ref[...] + _dot(wa, a2_ref[...]))
    if vres_refs is not None:
        vf_ref, v0_ref, v1_ref, v2_ref = vres_refs
        mix = jax.nn.sigmoid(v0_ref[...] + _dot(_dot(v, v1_ref[...]), v2_ref[...]))
        v = v + (vf_ref[...] - v) * mix
    ld = -jnp.exp(wlog)
    return r, ld, k, v, a, _dot(jax.nn.sigmoid(gl), g2_ref[...])


def _rw_level_masks(c):
    ti = lax.broadcasted_iota(I32, (c, c), 0)
    si = lax.broadcasted_iota(I32, (c, c), 1)
    masks = []
    n = 1
    while n < c:
        same = (ti // (2 * n)) == (si // (2 * n))
        masks.append(same & ((ti // n) % 2 == 1) & ((si // n) % 2 == 0))
        n *= 2
    return ti, si, masks


def _bdot(a, b, dims=_BNN):
    return lax.dot_general(a.astype(BF16), b.astype(BF16), dims, preferred_element_type=F32)


def _rw_kernel(*refs, tb, has_vres):
    n_prep = 12 if has_vres else 8
    vres_refs = refs[8:12] if has_vres else None
    kk_ref, ka_ref, rk_ref, lnw_ref, lnb_ref = refs[n_prep:n_prep + 5]
    outs = refs[n_prep + 5:]
    o_ref, st_ref = outs[0], outs[-1]
    r2d, ld2, k2d, v2d, a2d, g2d = _rw_prep_values(*refs[:8], vres_refs)
    if not has_vres:
        outs[1][...] = v2d
    c, nh, hd = RW_CHUNK, RW_HEADS, RW_HD
    nc = tb // c

    @pl.when(pl.program_id(0) == 0)
    def _():
        st_ref[...] = jnp.zeros_like(st_ref)

    def split(x):
        return jnp.stack([x[ci * c:(ci + 1) * c, h * hd:(h + 1) * hd] for ci in range(nc) for h in range(nh)])

    def split_row(ref):
        x = ref[...]
        return jnp.stack([x[:, h * hd:(h + 1) * hd] for _ in range(nc) for h in range(nh)])

    cw2 = _dot_sel(_block_tril(tb, c), ld2)
    r, ld, cw, k, v, a = split(r2d), split(ld2), split(cw2), split(k2d), split(v2d), split(a2d)
    kk_w, ka_w, rk_w, lnw, lnb = (split_row(p) for p in (kk_ref, ka_ref, rk_ref, lnw_ref, lnb_ref))
    ti, si, masks = _rw_level_masks(c)
    tril, strict = (ti >= si)[None], (ti > si)[None]

    wc = jnp.exp(cw)
    wprev = jnp.exp(cw - ld)
    winv = jnp.exp(-cw)
    wc_last = wc[:, c - 1:c, :]
    wtail = wc_last * winv
    kkf = k * kk_w
    kk = kkf / jnp.maximum(jnp.sqrt(jnp.sum(kkf * kkf, axis=-1, keepdims=True)), 1e-12)
    k2 = k * (1.0 + (a - 1.0) * ka_w)
    bv = kk * a
    at = -kk * wprev
    rt = r * wc
    ar = jnp.concatenate([at, rt], axis=1)
    gb = _bdot(ar, bv * winv, _BNT)
    gk = _bdot(ar, k2 * winv, _BNT)
    lab = jnp.where(strict, gb[:, :c], 0.0)
    lak = jnp.where(strict, gk[:, :c], 0.0)
    mrb = jnp.where(tril, gb[:, c:], 0.0)
    mrk = jnp.where(tril, gk[:, c:], 0.0)
    x = jnp.where((ti == si)[None], 1.0, 0.0) + jnp.where(masks[0][None], lab, 0.0)
    for m in masks[1:]:
        x = x + _bdot(_bdot(x, jnp.where(m[None], lab, 0.0)), x)
    lv = _bdot(jnp.concatenate([lak, mrk], axis=1), v)
    p = _bdot(x, at)
    q = _bdot(x, lv[:, :c])
    r2 = rt + _bdot(mrb, p)
    y0 = _bdot(mrb, q) + lv[:, c:]
    bh = bv * wtail
    g = _bdot(p, bh, _BTN)
    s_add = _bdot(q, bh, _BTN) + _bdot(v, k2 * wtail, _BTN)
    s = st_ref[...]
    entering = []
    for ci in range(nc):
        b0, b1 = ci * nh, (ci + 1) * nh
        entering.append(s)
        s = s * wc_last[b0:b1] + _bdot(s, g[b0:b1]) + s_add[b0:b1]
    st_ref[...] = s
    y = _bdot(r2, jnp.concatenate(entering, axis=0), _BNT) + y0
    mean = jnp.mean(y, axis=-1, keepdims=True)
    var = jnp.mean(jnp.square(y - mean), axis=-1, keepdims=True)
    yn = (y - mean) * lax.rsqrt(var + RW_LN_EPS) * lnw + lnb
    out = yn + jnp.sum(r * k2 * rk_w, axis=-1, keepdims=True) * v
    for ci in range(nc):
        rows = jnp.concatenate([out[ci * nh + h] for h in range(nh)], axis=1)
        o_ref[ci * c:(ci + 1) * c, :] = (rows * g2d[ci * c:(ci + 1) * c, :]).astype(o_ref.dtype)


def _rwkv7(cols, mu, w0, w2p, a0, a2p, g2, vres, k_k, k_a, r_k, ln_w, ln_b, tb=256):
    t, nc = cols.shape
    w = RW_W
    full = lambda shape: pl.BlockSpec(shape, lambda i: (0,) * len(shape))
    col = pl.BlockSpec((tb, w), lambda i: (i, 0))
    in_specs = [pl.BlockSpec((tb, nc), lambda i: (i, 0)),
                pl.BlockSpec((8, nc), lambda i: (jnp.maximum(i * (tb // 8) - 1, 0), 0)),
                full((1, nc)), full((1, w)), full((128, w)), full((1, w)), full((128, w)), full((128, w))]
    args = [cols, cols, mu, w0, w2p, a0, a2p, g2]
    if vres is not None:
        v_first, v0, v1p, v2p = vres
        in_specs += [col, full((1, w)), full((w, 128)), full((128, w))]
        args += [v_first, v0, v1p, v2p]
    in_specs += [full((1, w))] * 5
    args += [k_k, k_a, r_k, ln_w, ln_b]
    out_specs, out_shape = [col], [jax.ShapeDtypeStruct((t, w), BF16)]
    if vres is None:
        out_specs.append(col)
        out_shape.append(jax.ShapeDtypeStruct((t, w), F32))
    res = pl.pallas_call(
        functools.partial(_rw_kernel, tb=tb, has_vres=vres is not None),
        grid=(t // tb,),
        in_specs=in_specs,
        out_specs=out_specs,
        out_shape=out_shape,
        scratch_shapes=[pltpu.VMEM((RW_HEADS, RW_HD, RW_HD), F32)],
        compiler_params=_params("arbitrary"),
        name="rwkv7",
    )(*args)
    return (res[0], res[1]) if vres is None else (res[0], None)


def _fox_cum_kernel(h_ref, wt_ref, w_ref, fbc_ref, fbr_ref, row_ref, col_ref, crow_ref, ccol_ref):
    @pl.when(pl.program_id(0) == 0)
    def _():
        crow_ref[...] = jnp.zeros_like(crow_ref)
        ccol_ref[...] = jnp.zeros_like(ccol_ref)

    h = h_ref[...]
    tt = h.shape[0]
    ti = lax.broadcasted_iota(I32, (tt, tt), 0)
    si = lax.broadcasted_iota(I32, (tt, tt), 1)
    lf_r = jax.nn.log_sigmoid(_dot(wt_ref[...], h, _NT) + fbc_ref[...])
    cum_r = _dot_sel((ti <= si).astype(F32), lf_r, x_first=True) + crow_ref[:, 0:1]
    row_ref[...] = cum_r
    crow_ref[...] = jnp.broadcast_to(cum_r[:, tt - 1:tt], crow_ref.shape)
    lf_c = jax.nn.log_sigmoid(_dot(h, w_ref[...]) + fbr_ref[...])
    cum_c = _dot_sel((ti >= si).astype(F32), lf_c) + ccol_ref[0:1, :]
    col_ref[...] = cum_c
    ccol_ref[...] = jnp.broadcast_to(cum_c[tt - 1:tt, :], ccol_ref.shape)


def _fox_cum(h, w_ff_t, fb, tt=512):
    t, d = h.shape
    nh = FOX_HEADS
    full = lambda shape: pl.BlockSpec(shape, lambda i: (0,) * len(shape))
    return pl.pallas_call(
        _fox_cum_kernel,
        grid=(t // tt,),
        in_specs=[pl.BlockSpec((tt, d), lambda i: (i, 0)), full((nh, d)), full((d, nh)), full((nh, 1)), full((1, nh))],
        out_specs=[pl.BlockSpec((nh, tt), lambda i: (0, i)), pl.BlockSpec((tt, nh), lambda i: (i, 0))],
        out_shape=[jax.ShapeDtypeStruct((nh, t), F32), jax.ShapeDtypeStruct((t, nh), F32)],
        scratch_shapes=[pltpu.VMEM((nh, 128), F32), pltpu.VMEM((8, nh), F32)],
        compiler_params=_params("arbitrary"),
        name="fox_cum",
    )(h, w_ff_t, w_ff_t.T, fb.reshape(nh, 1), fb.reshape(1, nh))


def _fox_kernel(q_ref, k_ref, v_ref, cq_ref, ck_ref, o_ref, m_ref, l_ref, acc_ref, *, tq):
    hh, qi = pl.program_id(0), pl.program_id(1)
    nk = k_ref.shape[0] // tq
    q = q_ref[...]
    lane = lax.broadcasted_iota(I32, cq_ref.shape, 1)
    cq = jnp.sum(jnp.where(lane == hh, cq_ref[...], 0.0), axis=-1, keepdims=True)
    m_ref[...] = jnp.full_like(m_ref, -jnp.inf)
    l_ref[...] = jnp.zeros_like(l_ref)
    acc_ref[...] = jnp.zeros_like(acc_ref)

    log2e = 1.4426950408889634
    cq2 = cq * log2e

    def block(ki, diagonal):
        rows = pl.ds(pl.multiple_of(ki * tq, tq), tq)
        s = _dot(q, k_ref[rows, :], _NT) * (FOX_HD ** -0.5 * log2e) - ck_ref[pl.ds(hh * nk + ki, 1), :] * log2e
        if diagonal:
            s = jnp.where(lax.broadcasted_iota(I32, s.shape, 1) <= lax.broadcasted_iota(I32, s.shape, 0), s, -jnp.inf)
        m_old = m_ref[...]
        m_new = jnp.maximum(m_old, jnp.max(s, axis=-1, keepdims=True) + cq2)
        alpha = jnp.exp2(m_old - m_new)
        p = jnp.exp2(s - (m_new - cq2))
        l_ref[...] = alpha * l_ref[...] + jnp.sum(p, axis=-1, keepdims=True)
        acc_ref[...] = alpha * acc_ref[...] + _dot(p, v_ref[rows, :])
        m_ref[...] = m_new

    def body(ki, carry):
        block(ki, False)
        return carry

    lax.fori_loop(0, qi, body, 0)
    block(qi, True)
    o_ref[...] = (acc_ref[...] / l_ref[...]).astype(o_ref.dtype)


def _fox_attention(qkv, cum_row, cum_col, tq=2048):
    t = qkv.shape[0]
    h = FOX_HEADS
    nq = t // tq
    return pl.pallas_call(
        functools.partial(_fox_kernel, tq=tq),
        grid=(h, nq),
        in_specs=[pl.BlockSpec((tq, 128), lambda hh, qi: (qi, hh)),
                  pl.BlockSpec((t, 128), lambda hh, qi: (0, h + hh)),
                  pl.BlockSpec((t, 128), lambda hh, qi: (0, 2 * h + hh)),
                  pl.BlockSpec((tq, h), lambda hh, qi: (qi, 0)),
                  pl.BlockSpec((h * nq, tq), lambda hh, qi: (0, 0))],
        out_specs=pl.BlockSpec((tq, 128), lambda hh, qi: (qi, hh)),
        out_shape=jax.ShapeDtypeStruct((t, h * 128), BF16),
        scratch_shapes=[pltpu.VMEM((tq, 1), F32), pltpu.VMEM((tq, 1), F32), pltpu.VMEM((tq, 128), F32)],
        compiler_params=_params("parallel", "arbitrary"),
        name="fox_attention",
    )(qkv, qkv, qkv, cum_col, cum_row.reshape(h * nq, tq))


def _merge_kernel(*refs):
    y_refs, w_refs, g_refs, o_ref = refs[0:4], refs[4:8], refs[8:12], refs[12]
    acc = None
    for n in range(N_BRANCH):
        term = jax.nn.sigmoid(g_refs[n][...]) * _dot(y_refs[n][...], w_refs[n][...])
        acc = term if acc is None else acc + term
    o_ref[...] = acc.astype(o_ref.dtype)


def _merge(branches, w_branch, layer, gates, tm=1024, tn=512):
    t = branches[0].shape[0]
    d = D_MODEL
    nb = d // tn
    y_spec = pl.BlockSpec((tm, BRANCH_W), lambda i, j: (i, 0))
    w_specs = [pl.BlockSpec((None, None, BRANCH_W, tn), lambda i, j, n=n: (layer, n, 0, j)) for n in range(N_BRANCH)]
    g_specs = [pl.BlockSpec((tm, tn), lambda i, j, n=n: (i, n * nb + j)) for n in range(N_BRANCH)]
    return pl.pallas_call(
        _merge_kernel,
        grid=(t // tm, nb),
        in_specs=[y_spec] * 4 + w_specs + g_specs,
        out_specs=pl.BlockSpec((tm, tn), lambda i, j: (i, j)),
        out_shape=jax.ShapeDtypeStruct((t, d), BF16),
        compiler_params=_params("parallel", "parallel"),
        name="merge",
    )(*branches, *([w_branch] * N_BRANCH), *([gates] * N_BRANCH))


def _topk_rows(s, k, payloads=()):
    n = s.shape[0]
    row = lax.broadcasted_iota(I32, s.shape, 0).astype(F32)
    vals, rows, picked = [], [], [[] for _ in payloads]
    for _ in range(k):
        m = jnp.max(s, axis=0, keepdims=True)
        first = jnp.min(jnp.where(s == m, row, float(n)), axis=0, keepdims=True)
        sel = row == first
        vals.append(m)
        rows.append(first)
        for out, p in zip(picked, payloads):
            out.append(jnp.max(jnp.where(sel, p, -1.0), axis=0, keepdims=True))
        s = jnp.where(sel, -jnp.inf, s)
    cat = lambda xs: jnp.concatenate(xs, axis=0)
    return cat(vals), cat(rows), [cat(p) for p in picked]


_PEER_PAIRS = [(r1, r2) for r1 in range(PEER_TOPK) for r2 in range(PEER_TOPK) if (r1 + 1) * (r2 + 1) <= PEER_TOPK]


def _peer_topk_kernel(q_ref, keys_ref, a_ref, b_ref, g_ref):
    kq = PEER_TOPK
    half = PEER_DKEY // 2
    a_rows, b_rows, g_rows = [], [], []
    for hh in range(PEER_HEADS):
        top = []
        for p in range(2):
            qs = q_ref[:, (2 * hh + p) * half:(2 * hh + p + 1) * half]
            s = _dot(keys_ref[hh, p], qs, _NT)
            top.append(_topk_rows(s, kq)[:2])
        (v1, i1), (v2, i2) = top
        cand = jnp.concatenate([v1[r1:r1 + 1, :] + v2[r2:r2 + 1, :] for r1, r2 in _PEER_PAIRS], axis=0)
        key1 = jnp.concatenate([i1[r1:r1 + 1, :] for r1, _ in _PEER_PAIRS], axis=0)
        key2 = jnp.concatenate([i2[r2:r2 + 1, :] for _, r2 in _PEER_PAIRS], axis=0)
        best, _, (e1, e2) = _topk_rows(cand, kq, (key1, key2))
        ex = jnp.exp(best - best[0:1, :])
        a_rows.append(e1)
        b_rows.append(e2)
        g_rows.append(ex / jnp.sum(ex, axis=0, keepdims=True))
    a_ref[...] = jnp.concatenate(a_rows, axis=0).T.astype(I32)
    b_ref[...] = jnp.concatenate(b_rows, axis=0).T.astype(I32)
    g_ref[...] = jnp.concatenate(g_rows, axis=0).T


def _peer_topk(q, keys, tt=512):
    t = q.shape[0]
    n = PEER_HEADS * PEER_TOPK
    spec = pl.BlockSpec((tt, n), lambda i: (i, 0))
    return pl.pallas_call(
        _peer_topk_kernel,
        grid=(t // tt,),
        in_specs=[pl.BlockSpec((tt, q.shape[1]), lambda i: (i, 0)),
                  pl.BlockSpec(keys.shape, lambda i: (0, 0, 0, 0))],
        out_specs=[spec, spec, spec],
        out_shape=[jax.ShapeDtypeStruct((t, n), I32), jax.ShapeDtypeStruct((t, n), I32), jax.ShapeDtypeStruct((t, n), F32)],
        compiler_params=_params("parallel"),
        name="peer_topk",
    )(q, keys)


def _peer_weights_kernel(a_ref, b_ref, g_ref, o_ref):
    tt = a_ref.shape[0]
    nk = PEER_NKEYS
    sub = lax.broadcasted_iota(I32, (nk, a_ref.shape[1]), 0)

    group = 64

    def body(i, carry):
        t0 = pl.multiple_of(i * group, group)
        lefts, rights = [], []
        for u in range(group):
            a = a_ref[pl.ds(t0 + u, 1), :]
            b = b_ref[pl.ds(t0 + u, 1), :]
            g = g_ref[pl.ds(t0 + u, 1), :]
            g_hi = g.astype(BF16).astype(F32)
            g_lo = g - g_hi
            lefts.append(jnp.concatenate([jnp.where(sub == a, g_hi, 0.0), jnp.where(sub == a, g_lo, 0.0)], axis=1))
            onehot = jnp.where(sub == b, 1.0, 0.0)
            rights.append(jnp.concatenate([onehot, onehot], axis=1))
        o_ref[pl.ds(t0, group)] = _bdot(jnp.stack(lefts), jnp.stack(rights), _BNT)
        return carry

    lax.fori_loop(0, tt // group, body, 0)


def _peer_weights(a, b, g, tt=128):
    t, n = a.shape
    spec = pl.BlockSpec((tt, n), lambda i: (i, 0))
    return pl.pallas_call(
        _peer_weights_kernel,
        grid=(t // tt,),
        in_specs=[spec, spec, spec],
        out_specs=pl.BlockSpec((tt, PEER_NKEYS, PEER_NKEYS), lambda i: (i, 0, 0)),
        out_shape=jax.ShapeDtypeStruct((t, PEER_NKEYS, PEER_NKEYS), F32),
        compiler_params=_params("parallel"),
        name="peer_weights",
    )(a, b, g)


def _peer_act_kernel(h_ref, u_ref, w_ref, o_ref, *, ne1):
    tm = h_ref.shape[0]
    z = _dot(h_ref[...], u_ref[...], _NT)
    gelu = 0.5 * z * (1.0 + lax.erf(z * (2.0 ** -0.5)))
    w2 = w_ref.reshape(tm * ne1, PEER_NKEYS)
    w = jnp.concatenate([w2[pl.ds(j, tm, stride=ne1), :] for j in range(ne1)], axis=1)
    o_ref[...] = (gelu * w).astype(o_ref.dtype)


def _peer_act(h, u, layer, wts, tm=1024, te=1024):
    t, d = h.shape
    e = u.shape[1]
    ne1 = te // PEER_NKEYS
    return pl.pallas_call(
        functools.partial(_peer_act_kernel, ne1=ne1),
        grid=(t // tm, e // te),
        in_specs=[pl.BlockSpec((tm, d), lambda i, j: (i, 0), pipeline_mode=pl.Buffered(1)),
                  pl.BlockSpec((None, te, d), lambda i, j: (layer, j, 0)),
                  pl.BlockSpec((tm, ne1, PEER_NKEYS), lambda i, j: (i, j, 0))],
        out_specs=pl.BlockSpec((tm, te), lambda i, j: (i, j)),
        out_shape=jax.ShapeDtypeStruct((t, e), BF16),
        compiler_params=_params("parallel", "parallel"),
        name="peer_act",
    )(h, u, wts)


def kernel(x, c, w_in, w_branch, w_out, ada_w, ada_b, ada_table, hg_lb_logits, rw_mu, rw_w0, rw_w2, rw_a0, rw_a2, rw_g2, rw_k_k, rw_k_a, rw_r_k, rw_ln_w, rw_ln_b, rw_v0, rw_v1, rw_v2, fox_fb, peer_wq, peer_keys, peer_u, peer_v, final_norm_w):
    b, t, d = x.shape
    assert b == 1 and d == D_MODEL
    assert w_in.shape[2] == COL_GATE + N_BRANCH * d and COL_RW % 256 == 0 and COL_FOX % 256 == 0 and COL_FF % 256 == 0
    depth = w_in.shape[0]
    x = x.reshape(t, d)

    c8 = jnp.broadcast_to(c, (8, d))
    mods = _matmul(c8, ada_w, name="ada_mods", n_out=N_MOD * d, tm=8, tn=512, out_dtype=F32, pre="silu",
                   bias=ada_b.reshape(1, -1))[0].reshape(N_MOD, d)
    lbs = _hg_lower_bounds(hg_lb_logits)
    ret_tables = _retention_tables(t)
    w_in_t = jnp.swapaxes(w_in, 1, 2)
    w_ff_t = w_in_t[:, COL_FF:COL_GATE, :]
    u_bf, v_bf = peer_u.astype(BF16), peer_v.astype(BF16)
    zpad = lambda z, axis, n: jnp.pad(z, [(0, n - z.shape[i]) if i == axis else (0, 0) for i in range(z.ndim)])

    v_first = None
    for l in range(depth):
        m = mods + ada_table[l]
        sh1, sc1, g1, sh2, sc2, g2 = [m[i:i + 1] for i in range(N_MOD)]
        h = _rms_mod(x, 1.0 + sc1, sh1, BF16)
        p_rh = _matmul(h, w_in_t, name="proj_ret_hg", layer=l, wt_row0=0, n_out=COL_RW, tm=MM_TM, tn=MM_TN, out_dtype=F32)
        p_rw = _matmul(h, w_in_t, name="proj_rwkv", layer=l, wt_row0=COL_RW, n_out=RW_COLS, tm=MM_TM, tn=MM_TN // 2,
                       out_dtype=F32)
        p_fox = _matmul(h, w_in_t, name="proj_fox", layer=l, wt_row0=COL_FOX, n_out=COL_FF - COL_FOX, tm=MM_TM // 2,
                        tn=MM_TN, out_dtype=BF16)
        p_gate = _matmul(h, w_in_t, name="proj_gate", layer=l, wt_row0=COL_GATE, n_out=N_BRANCH * d, tm=MM_TM, tn=MM_TN,
                         out_dtype=F32)
        ya = _retention(p_rh, ret_tables)
        yb = _hgrn2(p_rh, lbs[l:l + 1])
        w2p = jnp.concatenate([rw_w2[l], jnp.zeros_like(rw_a2[l])], axis=0)
        a2p = jnp.concatenate([jnp.zeros_like(rw_w2[l]), rw_a2[l]], axis=0)
        vres = None
        if l > 0:
            vres = (v_first, rw_v0[l - 1][None], zpad(rw_v1[l - 1], 1, 128), zpad(rw_v2[l - 1], 0, 128))
        yc, v_own = _rwkv7(p_rw, rw_mu[l][None], rw_w0[l][None], w2p, rw_a0[l][None], a2p, rw_g2[l], vres, rw_k_k[l][None],
                           rw_k_a[l][None], rw_r_k[l].reshape(1, RW_W), rw_ln_w[l][None], rw_ln_b[l][None])
        if l == 0:
            v_first = v_own
        cum_row, cum_col = _fox_cum(h, w_ff_t[l], fox_fb[l])
        yd = _fox_attention(p_fox, cum_row, cum_col)
        merged = _merge([ya, yb, yc, yd], w_branch, l, p_gate)
        x = _matmul(merged, w_out, name="out_proj", layer=l, n_out=d, tm=MM_TM, tn=MM_TN, out_dtype=F32, res=x, gate=g1)
        h2 = _rms_mod(x, 1.0 + sc2, sh2, BF16)
        q = _matmul(h2, peer_wq, name="peer_query", layer=l, n_out=PEER_HEADS * PEER_DKEY, tm=MM_TM // 2, tn=MM_TN,
                    out_dtype=F32)
        ea, eb, eg = _peer_topk(q, peer_keys[l])
        wts = _peer_weights(ea, eb, eg)
        act = _peer_act(h2, u_bf, l, wts)
        x = _matmul_acc_res(act, v_bf, l, x, g2, tm=1024, tn=1024, tk=2048)
    out = _rms_mod(x, final_norm_w[None], jnp.zeros((1, d), F32), F32)
    return out.reshape(b, t, d)
```

```python
import functools

import jax
import jax.numpy as jnp
import numpy as np
from jax import lax
from jax.experimental import pallas as pl
from jax.experimental.pallas import tpu as pltpu

F32, BF16, I32 = jnp.float32, jnp.bfloat16, jnp.int32

D_MODEL = 4096
NORM_EPS = 1e-6
N_MOD = 6
RET_HEADS, RET_DK, RET_CHUNK, ROPE_BASE = 8, 128, 128, 10000.0
HG_HEADS, HG_DK = 8, 128
HG_SUB = 16
RW_HEADS, RW_HD = 16, 64
RW_W = RW_HEADS * RW_HD
RW_CHUNK = 64
RW_LN_EPS = 64e-5
FOX_HEADS, FOX_HD = 8, 128
N_BRANCH, BRANCH_W = 4, 1024
PEER_HEADS, PEER_NKEYS, PEER_DKEY, PEER_TOPK = 8, 128, 256, 16
PEER_EXPERTS = PEER_NKEYS * PEER_NKEYS
COL_RW = 8192
RW_COLS = 3 * RW_W + 64 + 64 + 128
COL_FOX = COL_RW + RW_COLS
COL_FF = COL_FOX + 3 * FOX_HEADS * FOX_HD
COL_GATE = COL_FF + FOX_HEADS

V7X_VMEM_LIMIT_BYTES = 56 * 1024 * 1024
MM_TM, MM_TN = 2048, 512

_NN = (((1,), (0,)), ((), ()))
_NT = (((1,), (1,)), ((), ()))
_TN = (((0,), (0,)), ((), ()))


def _dot(a, b, dims=_NN):
    return lax.dot_general(a.astype(BF16), b.astype(BF16), dims, preferred_element_type=F32)


_BNN = (((2,), (1,)), ((0,), (0,)))
_BNT = (((2,), (2,)), ((0,), (0,)))
_BTN = (((1,), (1,)), ((0,), (0,)))


def _block_tril(size, block):
    ti = lax.broadcasted_iota(I32, (size, size), 0)
    si = lax.broadcasted_iota(I32, (size, size), 1)
    return ((ti >= si) & (ti // block == si // block)).astype(F32)


def _dot_sel(sel, x, x_first=False):
    hi = x.astype(BF16)
    r1 = x - hi.astype(F32)
    mid = r1.astype(BF16)
    lo = (r1 - mid.astype(F32)).astype(BF16)
    s = sel.astype(BF16)
    out = None
    for part in (hi, mid, lo):
        ab = (part, s) if x_first else (s, part)
        term = lax.dot_general(*ab, _NN, preferred_element_type=F32)
        out = term if out is None else out + term
    return out


def _params(*sem):
    return pltpu.CompilerParams(dimension_semantics=sem, vmem_limit_bytes=V7X_VMEM_LIMIT_BYTES)


def _mm_kernel(*refs, pre, has_bias, has_res, w_transposed):
    a_ref, w_ref = refs[0], refs[1]
    k = 2
    a = a_ref[...]
    if pre == "silu":
        a = a * jax.nn.sigmoid(a)
    y = _dot(a, w_ref[0], _NT) if w_transposed else _dot(a, w_ref[...])
    if has_bias:
        y = y + refs[k][...]
        k += 1
    if has_res:
        y = refs[k][...] + refs[k + 1][...] * y
        k += 2
    refs[k][...] = y.astype(refs[k].dtype)


def _matmul(a, w, *, name, n_out, tm, tn, out_dtype, layer=None, col_block0=0, wt_row0=None, pre=None, bias=None, res=None,
            gate=None):
    m, kdim = a.shape
    assert m % tm == 0 and (wt_row0 is None or (n_out % tn == 0 and wt_row0 % 8 == 0))
    grid = (m // tm, pl.cdiv(n_out, tn))
    if wt_row0 is not None:
        w_spec = pl.BlockSpec((pl.Element(1), pl.Element(tn), pl.Element(kdim)),
                              lambda i, j: (layer, pl.multiple_of(wt_row0 + j * tn, 8), 0))
    elif layer is None:
        w_spec = pl.BlockSpec((kdim, tn), lambda i, j: (0, j + col_block0))
    else:
        w_spec = pl.BlockSpec((None, kdim, tn), lambda i, j: (layer, 0, j + col_block0))
    a_mode = pl.Buffered(1) if grid[1] >= 8 else None
    in_specs = [pl.BlockSpec((tm, kdim), lambda i, j: (i, 0), pipeline_mode=a_mode), w_spec]
    args = [a, w]
    if bias is not None:
        in_specs.append(pl.BlockSpec((1, tn), lambda i, j: (0, j)))
        args.append(bias)
    if res is not None:
        in_specs += [pl.BlockSpec((tm, tn), lambda i, j: (i, j)), pl.BlockSpec((1, tn), lambda i, j: (0, j))]
        args += [res, gate]
    return pl.pallas_call(
        functools.partial(_mm_kernel, pre=pre, has_bias=bias is not None, has_res=res is not None,
                          w_transposed=wt_row0 is not None),
        grid=grid,
        in_specs=in_specs,
        out_specs=pl.BlockSpec((tm, tn), lambda i, j: (i, j)),
        out_shape=jax.ShapeDtypeStruct((m, n_out), out_dtype),
        compiler_params=_params("parallel", "parallel"),
        name=name,
    )(*args)


def _mm_acc_kernel(a_ref, w_ref, x_ref, g_ref, o_ref, acc_ref):
    @pl.when(pl.program_id(2) == 0)
    def _():
        acc_ref[...] = jnp.zeros_like(acc_ref)

    acc_ref[...] += _dot(a_ref[...], w_ref[...])

    @pl.when(pl.program_id(2) == pl.num_programs(2) - 1)
    def _():
        o_ref[...] = x_ref[...] + g_ref[...] * acc_ref[...]


def _matmul_acc_res(a, w, layer, res, gate, *, tm, tn, tk):
    m, kdim = a.shape
    n = w.shape[2]
    assert kdim % tk == 0 and m % tm == 0 and n % tn == 0
    return pl.pallas_call(
        _mm_acc_kernel,
        grid=(m // tm, n // tn, kdim // tk),
        in_specs=[
            pl.BlockSpec((tm, tk), lambda i, j, k: (i, k)),
            pl.BlockSpec((None, tk, tn), lambda i, j, k: (layer, k, j)),
            pl.BlockSpec((tm, tn), lambda i, j, k: (i, j)),
            pl.BlockSpec((1, tn), lambda i, j, k: (0, j)),
        ],
        out_specs=pl.BlockSpec((tm, tn), lambda i, j, k: (i, j)),
        out_shape=jax.ShapeDtypeStruct((m, n), F32),
        scratch_shapes=[pltpu.VMEM((tm, tn), F32)],
        compiler_params=_params("parallel", "parallel", "arbitrary"),
        name="peer_out",
    )(a, w, res, gate)


def _norm_kernel(x_ref, mul_ref, add_ref, o_ref):
    x = x_ref[...]
    ms = jnp.mean(x * x, axis=-1, keepdims=True)
    o_ref[...] = (x * lax.rsqrt(ms + NORM_EPS) * mul_ref[...] + add_ref[...]).astype(o_ref.dtype)


def _rms_mod(x, mul, add, out_dtype, tm=256):
    m, d = x.shape
    return pl.pallas_call(
        _norm_kernel,
        grid=(m // tm,),
        in_specs=[pl.BlockSpec((tm, d), lambda i: (i, 0)), pl.BlockSpec((1, d), lambda i: (0, 0)),
                  pl.BlockSpec((1, d), lambda i: (0, 0))],
        out_specs=pl.BlockSpec((tm, d), lambda i: (i, 0)),
        out_shape=jax.ShapeDtypeStruct((m, d), out_dtype),
        compiler_params=_params("parallel"),
        name="rms_mod",
    )(x, mul, add)


def _hg_lb_kernel(lg_ref, o_ref):
    lg = lg_ref[...]
    e = jnp.exp(lg - jnp.max(lg, axis=0, keepdims=True))
    p = e / jnp.sum(e, axis=0, keepdims=True)
    rows = [jnp.zeros_like(p[0:1])]
    for l in range(1, lg.shape[0]):
        rows.append(rows[-1] + p[l:l + 1])
    o_ref[...] = jnp.concatenate(rows, axis=0)


def _hg_lower_bounds(logits):
    return pl.pallas_call(_hg_lb_kernel, out_shape=jax.ShapeDtypeStruct(logits.shape, F32), name="hg_lb")(logits)


def _ret_body(q_ref, k_ref, v_ref, g_ref, cos_ref, sin_ref, dmat_ref, xi_ref, zeta_ref, gam_ref, o_ref, st_ref, first):
    c = RET_CHUNK
    dmat, xi, zeta, gam = dmat_ref[...], xi_ref[...], zeta_ref[...], gam_ref[...]
    st = jnp.where(first, 0.0, st_ref[...])
    for ci in range(q_ref.shape[0] // c):
        sl = slice(ci * c, (ci + 1) * c)
        q, k, v = q_ref[sl, :], k_ref[sl, :], v_ref[sl, :]
        cos, sin = cos_ref[sl, :], sin_ref[sl, :]
        half = RET_DK // 2
        qr = q * cos + pltpu.roll(q, half, 1) * sin
        kr = (k * cos + pltpu.roll(k, half, 1) * sin) * (RET_DK ** -0.5)
        scores = _dot(qr, kr, _NT) * dmat
        o = _dot(scores, v) + _dot(qr * xi, st)
        st = st * gam + _dot(kr * zeta, v, _TN)
        ms = jnp.mean(o * o, axis=-1, keepdims=True)
        g = g_ref[sl, :]
        y = o * lax.rsqrt(ms + NORM_EPS) * (g * jax.nn.sigmoid(g))
        o_ref[sl, :] = y.astype(o_ref.dtype)
    st_ref[...] = st


def _ret_kernel(*refs):
    _ret_body(*refs, pl.program_id(1) == 0)


def _retention(proj, tables, tb=2048):
    t = proj.shape[0]
    h = RET_HEADS
    cos, sin, dmat, xi, zeta, gam = tables
    c = RET_CHUNK
    col = lambda g: pl.BlockSpec((tb, 128), lambda hh, n, g=g: (n, g * h + hh))
    tab = pl.BlockSpec((tb, 128), lambda hh, n: (n, 0))
    return pl.pallas_call(
        _ret_kernel,
        grid=(h, t // tb),
        in_specs=[col(0), col(1), col(2), col(3), tab, tab,
                  pl.BlockSpec((None, c, c), lambda hh, n: (hh, 0, 0)),
                  pl.BlockSpec((None, c, 128), lambda hh, n: (hh, 0, 0)),
                  pl.BlockSpec((None, c, 128), lambda hh, n: (hh, 0, 0)),
                  pl.BlockSpec((None, 1, 128), lambda hh, n: (hh, 0, 0))],
        out_specs=pl.BlockSpec((tb, 128), lambda hh, n: (n, hh)),
        out_shape=jax.ShapeDtypeStruct((t, h * 128), BF16),
        scratch_shapes=[pltpu.VMEM((RET_DK, 128), F32)],
        compiler_params=_params("parallel", "arbitrary"),
        name="retention",
    )(proj, proj, proj, proj, cos, sin, dmat, xi, zeta, gam)


def _retention_tables(t):
    half = RET_DK // 2
    inv = ROPE_BASE ** (-jnp.arange(half, dtype=F32) / half)
    ang = jnp.arange(t, dtype=F32)[:, None] * inv[None, :]
    cos, sin = jnp.cos(ang), jnp.sin(ang)
    cos2 = jnp.concatenate([cos, cos], axis=-1)
    sin2 = jnp.concatenate([-sin, sin], axis=-1)
    c = RET_CHUNK
    log_g = jnp.log(1.0 - 2.0 ** (-5.0 - jnp.arange(RET_HEADS, dtype=F32)))
    pos = jnp.arange(c, dtype=F32)
    rel = pos[:, None] - pos[None, :]
    dmat = jnp.where(rel >= 0, jnp.exp(log_g[:, None, None] * jnp.maximum(rel, 0.0)), 0.0)
    xi = jnp.broadcast_to(jnp.exp(log_g[:, None] * (pos + 1.0))[:, :, None], (RET_HEADS, c, 128))
    zeta = jnp.broadcast_to(jnp.exp(log_g[:, None] * (c - 1.0 - pos))[:, :, None], (RET_HEADS, c, 128))
    gam = jnp.broadcast_to(jnp.exp(log_g * c)[:, None, None], (RET_HEADS, 1, 128))
    return cos2, sin2, dmat, xi, zeta, gam


def _hg_body(q_ref, f_ref, v_ref, g_ref, lb_ref, o_ref, st_ref, first):
    n = HG_SUB
    tb = q_ref.shape[0]
    nb = tb // n
    lb = lb_ref[...]
    fg = lb + (1.0 - lb) * jax.nn.sigmoid(f_ref[...])
    kk = (1.0 - fg).reshape(nb, n, 128)
    cum = _dot_sel(_block_tril(tb, n), jnp.log(fg)).reshape(nb, n, 128)
    q, v = q_ref[...].reshape(nb, n, 128), v_ref[...].reshape(nb, n, 128)
    row = lax.broadcasted_iota(I32, (nb, n, 128), 1)
    o = jnp.zeros((nb, n, 128), F32)
    for s in range(n):
        d = jnp.where(row >= s, cum - cum[:, s:s + 1, :], -jnp.inf)
        col = jnp.sum(q * jnp.exp(d) * kk[:, s:s + 1, :], axis=-1, keepdims=True)
        o = o + col * v[:, s:s + 1, :]
    last = cum[:, n - 1:n, :]
    upd = lax.dot_general(v.astype(BF16), (kk * jnp.exp(last - cum)).astype(BF16), _BTN, preferred_element_type=F32)
    dec = jnp.exp(last)
    st = jnp.where(first, 0.0, st_ref[...])
    entering = []
    for j in range(nb):
        entering.append(st)
        st = st * dec[j] + upd[j]
    st_ref[...] = st
    o = o + lax.dot_general((q * jnp.exp(cum)).astype(BF16), jnp.stack(entering).astype(BF16), _BNT,
                            preferred_element_type=F32)
    o = o.reshape(tb, 128)
    ms = jnp.mean(o * o, axis=-1, keepdims=True)
    g = g_ref[...]
    o_ref[...] = (o * lax.rsqrt(ms + NORM_EPS) * (g * jax.nn.sigmoid(g))).astype(o_ref.dtype)


def _hg_kernel(*refs):
    _hg_body(*refs, pl.program_id(1) == 0)


def _hgrn2(proj, lb, tb=256):
    t = proj.shape[0]
    h = HG_HEADS
    col = lambda g: pl.BlockSpec((tb, 128), lambda hh, n, g=g: (n, (4 + g) * h + hh))
    return pl.pallas_call(
        _hg_kernel,
        grid=(h, t // tb),
        in_specs=[col(0), col(1), col(2), col(3), pl.BlockSpec((1, 128), lambda hh, n: (0, hh))],
        out_specs=pl.BlockSpec((tb, 128), lambda hh, n: (n, hh)),
        out_shape=jax.ShapeDtypeStruct((t, h * 128), BF16),
        scratch_shapes=[pltpu.VMEM((128, HG_DK), F32)],
        compiler_params=_params("parallel", "arbitrary"),
        name="hgrn2",
    )(proj, proj, proj, proj, lb)


def _ret_hg_kernel(rq, rk, rv, rg, cos, sin, dmat, xi, zeta, gam, hq, hf, hv, hg, lb, ya_ref, yb_ref, rst_ref, hst_ref, *, hb):
    first = pl.program_id(1) == 0
    _ret_body(rq, rk, rv, rg, cos, sin, dmat, xi, zeta, gam, ya_ref, rst_ref, first)
    for j in range(rq.shape[0] // hb):
        sl = pl.ds(j * hb, hb)
        _hg_body(hq.at[sl], hf.at[sl], hv.at[sl], hg.at[sl], lb, yb_ref.at[sl], hst_ref, first if j == 0 else False)


def _retention_hgrn2(proj, tables, lb, tb=2048, hb=256):
    t = proj.shape[0]
    h = RET_HEADS
    cos, sin, dmat, xi, zeta, gam = tables
    c = RET_CHUNK
    col = lambda g: pl.BlockSpec((tb, 128), lambda hh, n, g=g: (n, g * h + hh))
    tab = pl.BlockSpec((tb, 128), lambda hh, n: (n, 0))
    per_head = lambda shape: pl.BlockSpec((None,) + shape, lambda hh, n: (hh, 0, 0))
    y_spec = pl.BlockSpec((tb, 128), lambda hh, n: (n, hh))
    y_shape = jax.ShapeDtypeStruct((t, h * 128), BF16)
    return pl.pallas_call(
        functools.partial(_ret_hg_kernel, hb=hb),
        grid=(h, t // tb),
        in_specs=[col(0), col(1), col(2), col(3), tab, tab, per_head((c, c)), per_head((c, 128)), per_head((c, 128)),
                  per_head((1, 128)), col(4), col(5), col(6), col(7), pl.BlockSpec((1, 128), lambda hh, n: (0, hh))],
        out_specs=[y_spec, y_spec],
        out_shape=[y_shape, y_shape],
        scratch_shapes=[pltpu.VMEM((RET_DK, 128), F32), pltpu.VMEM((128, HG_DK), F32)],
        compiler_params=_params("parallel", "arbitrary"),
        name="retention_hgrn2",
    )(*([proj] * 4), cos, sin, dmat, xi, zeta, gam, *([proj] * 4), lb)


def _rw_prep_values(cur_ref, prv_ref, mu_ref, w0_ref, w2_ref, a0_ref, a2_ref, g2_ref, vres_refs):
    cur = cur_ref[...]
    prev_row = jnp.where(pl.program_id(0) == 0, 0.0, prv_ref[7:8, :])
    row = lax.broadcasted_iota(I32, cur.shape, 0)
    shifted = jnp.where(row == 0, prev_row, pltpu.roll(cur, 1, 0))
    cols = cur + (shifted - cur) * mu_ref[...]
    w = RW_W
    r, k, v = cols[:, 0:w], cols[:, w:2 * w], cols[:, 2 * w:3 * w]
    wa = cols[:, 3 * w:3 * w + 128]
    lane = lax.broadcasted_iota(I32, wa.shape, 1)
    wa = jnp.where(lane < 64, jnp.tanh(wa), wa)
    gl = cols[:, 3 * w + 128:3 * w + 256]
    wlog = -jax.nn.softplus(-(w0_ref[...] + _dot(wa, w2_ref[...]))) - 0.5
    a = jax.nn.sigmoid(a0_ref[...] + _dot(wa, a2_ref[...]))
    if vres_refs is not None:
        vf_ref, v0_ref, v1_ref, v2_ref = vres_refs
        mix = jax.nn.sigmoid(v0_ref[...] + _dot(_dot(v, v1_ref[...]), v2_ref[...]))
        v = v + (vf_ref[...] - v) * mix
    ld = -jnp.exp(wlog)
    return r, ld, k, v, a, _dot(jax.nn.sigmoid(gl), g2_ref[...])


def _rw_level_masks(c):
    ti = lax.broadcasted_iota(I32, (c, c), 0)
    si = lax.broadcasted_iota(I32, (c, c), 1)
    masks = []
    n = 1
    while n < c:
        same = (ti // (2 * n)) == (si // (2 * n))
        masks.append(same & ((ti // n) % 2 == 1) & ((si // n) % 2 == 0))
        n *= 2
    return ti, si, masks


def _bdot(a, b, dims=_BNN):
    return lax.dot_general(a.astype(BF16), b.astype(BF16), dims, preferred_element_type=F32)


def _rw_kernel(*refs, tb, has_vres):
    n_prep = 12 if has_vres else 8
    vres_refs = refs[8:12] if has_vres else None
    kk_ref, ka_ref, rk_ref, lnw_ref, lnb_ref = refs[n_prep:n_prep + 5]
    outs = refs[n_prep + 5:]
    o_ref, st_ref = outs[0], outs[-1]
    r2d, ld2, k2d, v2d, a2d, g2d = _rw_prep_values(*refs[:8], vres_refs)
    if not has_vres:
        outs[1][...] = v2d
    c, nh, hd = RW_CHUNK, RW_HEADS, RW_HD
    nc = tb // c

    @pl.when(pl.program_id(0) == 0)
    def _():
        st_ref[...] = jnp.zeros_like(st_ref)

    def split(x):
        return jnp.stack([x[ci * c:(ci + 1) * c, h * hd:(h + 1) * hd] for ci in range(nc) for h in range(nh)])

    def split_row(ref):
        x = ref[...]
        return jnp.stack([x[:, h * hd:(h + 1) * hd] for _ in range(nc) for h in range(nh)])

    cw2 = _dot_sel(_block_tril(tb, c), ld2)
    r, ld, cw, k, v, a = split(r2d), split(ld2), split(cw2), split(k2d), split(v2d), split(a2d)
    kk_w, ka_w, rk_w, lnw, lnb = (split_row(p) for p in (kk_ref, ka_ref, rk_ref, lnw_ref, lnb_ref))
    ti, si, masks = _rw_level_masks(c)
    tril, strict = (ti >= si)[None], (ti > si)[None]

    wc = jnp.exp(cw)
    wprev = jnp.exp(cw - ld)
    winv = jnp.exp(-cw)
    wc_last = wc[:, c - 1:c, :]
    wtail = wc_last * winv
    kkf = k * kk_w
    kk = kkf / jnp.maximum(jnp.sqrt(jnp.sum(kkf * kkf, axis=-1, keepdims=True)), 1e-12)
    k2 = k * (1.0 + (a - 1.0) * ka_w)
    bv = kk * a
    at = -kk * wprev
    rt = r * wc
    ar = jnp.concatenate([at, rt], axis=1)
    gb = _bdot(ar, bv * winv, _BNT)
    gk = _bdot(ar, k2 * winv, _BNT)
    lab = jnp.where(strict, gb[:, :c], 0.0)
    lak = jnp.where(strict, gk[:, :c], 0.0)
    mrb = jnp.where(tril, gb[:, c:], 0.0)
    mrk = jnp.where(tril, gk[:, c:], 0.0)
    x = jnp.where((ti == si)[None], 1.0, 0.0) + jnp.where(masks[0][None], lab, 0.0)
    for m in masks[1:]:
        x = x + _bdot(_bdot(x, jnp.where(m[None], lab, 0.0)), x)
    lv = _bdot(jnp.concatenate([lak, mrk], axis=1), v)
    p = _bdot(x, at)
    q = _bdot(x, lv[:, :c])
    r2 = rt + _bdot(mrb, p)
    y0 = _bdot(mrb, q) + lv[:, c:]
    bh = bv * wtail
    g = _bdot(p, bh, _BTN)
    s_add = _bdot(q, bh, _BTN) + _bdot(v, k2 * wtail, _BTN)
    s = st_ref[...]
    entering = []
    for ci in range(nc):
        b0, b1 = ci * nh, (ci + 1) * nh
        entering.append(s)
        s = s * wc_last[b0:b1] + _bdot(s, g[b0:b1]) + s_add[b0:b1]
    st_ref[...] = s
    y = _bdot(r2, jnp.concatenate(entering, axis=0), _BNT) + y0
    mean = jnp.mean(y, axis=-1, keepdims=True)
    var = jnp.mean(jnp.square(y - mean), axis=-1, keepdims=True)
    yn = (y - mean) * lax.rsqrt(var + RW_LN_EPS) * lnw + lnb
    out = yn + jnp.sum(r * k2 * rk_w, axis=-1, keepdims=True) * v
    for ci in range(nc):
        rows = jnp.concatenate([out[ci * nh + h] for h in range(nh)], axis=1)
        o_ref[ci * c:(ci + 1) * c, :] = (rows * g2d[ci * c:(ci + 1) * c, :]).astype(o_ref.dtype)


def _rwkv7(cols, mu, w0, w2p, a0, a2p, g2, vres, k_k, k_a, r_k, ln_w, ln_b, tb=256):
    t, nc = cols.shape
    w = RW_W
    full = lambda shape: pl.BlockSpec(shape, lambda i: (0,) * len(shape))
    col = pl.BlockSpec((tb, w), lambda i: (i, 0))
    in_specs = [pl.BlockSpec((tb, nc), lambda i: (i, 0)),
                pl.BlockSpec((8, nc), lambda i: (jnp.maximum(i * (tb // 8) - 1, 0), 0)),
                full((1, nc)), full((1, w)), full((128, w)), full((1, w)), full((128, w)), full((128, w))]
    args = [cols, cols, mu, w0, w2p, a0, a2p, g2]
    if vres is not None:
        v_first, v0, v1p, v2p = vres
        in_specs += [col, full((1, w)), full((w, 128)), full((128, w))]
        args += [v_first, v0, v1p, v2p]
    in_specs += [full((1, w))] * 5
    args += [k_k, k_a, r_k, ln_w, ln_b]
    out_specs, out_shape = [col], [jax.ShapeDtypeStruct((t, w), BF16)]
    if vres is None:
        out_specs.append(col)
        out_shape.append(jax.ShapeDtypeStruct((t, w), F32))
    res = pl.pallas_call(
        functools.partial(_rw_kernel, tb=tb, has_vres=vres is not None),
        grid=(t // tb,),
        in_specs=in_specs,
        out_specs=out_specs,
        out_shape=out_shape,
        scratch_shapes=[pltpu.VMEM((RW_HEADS, RW_HD, RW_HD), F32)],
        compiler_params=_params("arbitrary"),
        name="rwkv7",
    )(*args)
    return (res[0], res[1]) if vres is None else (res[0], None)


def _fox_cum_kernel(h_ref, wt_ref, w_ref, fbc_ref, fbr_ref, row_ref, col_ref, crow_ref, ccol_ref):
    @pl.when(pl.program_id(0) == 0)
    def _():
        crow_ref[...] = jnp.zeros_like(crow_ref)
        ccol_ref[...] = jnp.zeros_like(ccol_ref)

    h = h_ref[...]
    tt = h.shape[0]
    ti = lax.broadcasted_iota(I32, (tt, tt), 0)
    si = lax.broadcasted_iota(I32, (tt, tt), 1)
    lf_r = jax.nn.log_sigmoid(_dot(wt_ref[...], h, _NT) + fbc_ref[...])
    cum_r = _dot_sel((ti <= si).astype(F32), lf_r, x_first=True) + crow_ref[:, 0:1]
    row_ref[...] = cum_r
    crow_ref[...] = jnp.broadcast_to(cum_r[:, tt - 1:tt], crow_ref.shape)
    lf_c = jax.nn.log_sigmoid(_dot(h, w_ref[...]) + fbr_ref[...])
    cum_c = _dot_sel((ti >= si).astype(F32), lf_c) + ccol_ref[0:1, :]
    col_ref[...] = cum_c
    ccol_ref[...] = jnp.broadcast_to(cum_c[tt - 1:tt, :], ccol_ref.shape)


def _fox_cum(h, w_ff_t, fb, tt=512):
    t, d = h.shape
    nh = FOX_HEADS
    full = lambda shape: pl.BlockSpec(shape, lambda i: (0,) * len(shape))
    return pl.pallas_call(
        _fox_cum_kernel,
        grid=(t // tt,),
        in_specs=[pl.BlockSpec((tt, d), lambda i: (i, 0)), full((nh, d)), full((d, nh)), full((nh, 1)), full((1, nh))],
        out_specs=[pl.BlockSpec((nh, tt), lambda i: (0, i)), pl.BlockSpec((tt, nh), lambda i: (i, 0))],
        out_shape=[jax.ShapeDtypeStruct((nh, t), F32), jax.ShapeDtypeStruct((t, nh), F32)],
        scratch_shapes=[pltpu.VMEM((nh, 128), F32), pltpu.VMEM((8, nh), F32)],
        compiler_params=_params("arbitrary"),
        name="fox_cum",
    )(h, w_ff_t, w_ff_t.T, fb.reshape(nh, 1), fb.reshape(1, nh))


def _fox_kernel(q_ref, k_ref, v_ref, cq_ref, ck_ref, o_ref, m_ref, l_ref, acc_ref, *, tq):
    hh, qi = pl.program_id(0), pl.program_id(1)
    nk = k_ref.shape[0] // tq
    q = q_ref[...]
    lane = lax.broadcasted_iota(I32, cq_ref.shape, 1)
    cq = jnp.sum(jnp.where(lane == hh, cq_ref[...], 0.0), axis=-1, keepdims=True)
    m_ref[...] = jnp.full_like(m_ref, -jnp.inf)
    l_ref[...] = jnp.zeros_like(l_ref)
    acc_ref[...] = jnp.zeros_like(acc_ref)

    log2e = 1.4426950408889634
    cq2 = cq * log2e

    def block(ki, diagonal):
        rows = pl.ds(pl.multiple_of(ki * tq, tq), tq)
        s = _dot(q, k_ref[rows, :], _NT) * (FOX_HD ** -0.5 * log2e) - ck_ref[pl.ds(hh * nk + ki, 1), :] * log2e
        if diagonal:
            s = jnp.where(lax.broadcasted_iota(I32, s.shape, 1) <= lax.broadcasted_iota(I32, s.shape, 0), s, -jnp.inf)
        m_old = m_ref[...]
        m_new = jnp.maximum(m_old, jnp.max(s, axis=-1, keepdims=True) + cq2)
        alpha = jnp.exp2(m_old - m_new)
        p = jnp.exp2(s - (m_new - cq2))
        l_ref[...] = alpha * l_ref[...] + jnp.sum(p, axis=-1, keepdims=True)
        acc_ref[...] = alpha * acc_ref[...] + _dot(p, v_ref[rows, :])
        m_ref[...] = m_new

    def body(ki, carry):
        block(ki, False)
        return carry

    lax.fori_loop(0, qi, body, 0)
    block(qi, True)
    o_ref[...] = (acc_ref[...] / l_ref[...]).astype(o_ref.dtype)


def _fox_attention(qkv, cum_row, cum_col, tq=2048):
    t = qkv.shape[0]
    h = FOX_HEADS
    nq = t // tq
    return pl.pallas_call(
        functools.partial(_fox_kernel, tq=tq),
        grid=(h, nq),
        in_specs=[pl.BlockSpec((tq, 128), lambda hh, qi: (qi, hh)),
                  pl.BlockSpec((t, 128), lambda hh, qi: (0, h + hh)),
                  pl.BlockSpec((t, 128), lambda hh, qi: (0, 2 * h + hh)),
                  pl.BlockSpec((tq, h), lambda hh, qi: (qi, 0)),
                  pl.BlockSpec((h * nq, tq), lambda hh, qi: (0, 0))],
        out_specs=pl.BlockSpec((tq, 128), lambda hh, qi: (qi, hh)),
        out_shape=jax.ShapeDtypeStruct((t, h * 128), BF16),
        scratch_shapes=[pltpu.VMEM((tq, 1), F32), pltpu.VMEM((tq, 1), F32), pltpu.VMEM((tq, 128), F32)],
        compiler_params=_params("parallel", "arbitrary"),
        name="fox_attention",
    )(qkv, qkv, qkv, cum_col, cum_row.reshape(h * nq, tq))


def _merge_kernel(*refs):
    y_refs, w_refs, g_refs, o_ref = refs[0:4], refs[4:8], refs[8:12], refs[12]
    acc = None
    for n in range(N_BRANCH):
        term = jax.nn.sigmoid(g_refs[n][...]) * _dot(y_refs[n][...], w_refs[n][...])
        acc = term if acc is None else acc + term
    o_ref[...] = acc.astype(o_ref.dtype)


def _merge(branches, w_branch, layer, gates, tm=1024, tn=512):
    t = branches[0].shape[0]
    d = D_MODEL
    nb = d // tn
    y_spec = pl.BlockSpec((tm, BRANCH_W), lambda i, j: (i, 0))
    w_specs = [pl.BlockSpec((None, None, BRANCH_W, tn), lambda i, j, n=n: (layer, n, 0, j)) for n in range(N_BRANCH)]
    g_specs = [pl.BlockSpec((tm, tn), lambda i, j, n=n: (i, n * nb + j)) for n in range(N_BRANCH)]
    return pl.pallas_call(
        _merge_kernel,
        grid=(t // tm, nb),
        in_specs=[y_spec] * 4 + w_specs + g_specs,
        out_specs=pl.BlockSpec((tm, tn), lambda i, j: (i, j)),
        out_shape=jax.ShapeDtypeStruct((t, d), BF16),
        compiler_params=_params("parallel", "parallel"),
        name="merge",
    )(*branches, *([w_branch] * N_BRANCH), *([gates] * N_BRANCH))


def _topk_rows(s, k, payloads=()):
    n = s.shape[0]
    row = lax.broadcasted_iota(I32, s.shape, 0).astype(F32)
    vals, rows, picked = [], [], [[] for _ in payloads]
    for _ in range(k):
        m = jnp.max(s, axis=0, keepdims=True)
        first = jnp.min(jnp.where(s == m, row, float(n)), axis=0, keepdims=True)
        sel = row == first
        vals.append(m)
        rows.append(first)
        for out, p in zip(picked, payloads):
            out.append(jnp.max(jnp.where(sel, p, -1.0), axis=0, keepdims=True))
        s = jnp.where(sel, -jnp.inf, s)
    cat = lambda xs: jnp.concatenate(xs, axis=0)
    return cat(vals), cat(rows), [cat(p) for p in picked]


_PEER_PAIRS = [(r1, r2) for r1 in range(PEER_TOPK) for r2 in range(PEER_TOPK) if (r1 + 1) * (r2 + 1) <= PEER_TOPK]


def _peer_topk_kernel(q_ref, keys_ref, a_ref, b_ref, g_ref):
    kq = PEER_TOPK
    half = PEER_DKEY // 2
    a_rows, b_rows, g_rows = [], [], []
    for hh in range(PEER_HEADS):
        top = []
        for p in range(2):
            qs = q_ref[:, (2 * hh + p) * half:(2 * hh + p + 1) * half]
            s = _dot(keys_ref[hh, p], qs, _NT)
            top.append(_topk_rows(s, kq)[:2])
        (v1, i1), (v2, i2) = top
        cand = jnp.concatenate([v1[r1:r1 + 1, :] + v2[r2:r2 + 1, :] for r1, r2 in _PEER_PAIRS], axis=0)
        key1 = jnp.concatenate([i1[r1:r1 + 1, :] for r1, _ in _PEER_PAIRS], axis=0)
        key2 = jnp.concatenate([i2[r2:r2 + 1, :] for _, r2 in _PEER_PAIRS], axis=0)
        best, _, (e1, e2) = _topk_rows(cand, kq, (key1, key2))
        ex = jnp.exp(best - best[0:1, :])
        a_rows.append(e1)
        b_rows.append(e2)
        g_rows.append(ex / jnp.sum(ex, axis=0, keepdims=True))
    a_ref[...] = jnp.concatenate(a_rows, axis=0).T.astype(I32)
    b_ref[...] = jnp.concatenate(b_rows, axis=0).T.astype(I32)
    g_ref[...] = jnp.concatenate(g_rows, axis=0).T


def _peer_topk(q, keys, tt=512):
    t = q.shape[0]
    n = PEER_HEADS * PEER_TOPK
    spec = pl.BlockSpec((tt, n), lambda i: (i, 0))
    return pl.pallas_call(
        _peer_topk_kernel,
        grid=(t // tt,),
        in_specs=[pl.BlockSpec((tt, q.shape[1]), lambda i: (i, 0)),
                  pl.BlockSpec(keys.shape, lambda i: (0, 0, 0, 0))],
        out_specs=[spec, spec, spec],
        out_shape=[jax.ShapeDtypeStruct((t, n), I32), jax.ShapeDtypeStruct((t, n), I32), jax.ShapeDtypeStruct((t, n), F32)],
        compiler_params=_params("parallel"),
        name="peer_topk",
    )(q, keys)


def _peer_weights_kernel(a_ref, b_ref, g_ref, o_ref):
    tt = a_ref.shape[0]
    nk = PEER_NKEYS
    sub = lax.broadcasted_iota(I32, (nk, a_ref.shape[1]), 0)

    group = 64

    def body(i, carry):
        t0 = pl.multiple_of(i * group, group)
        lefts, rights = [], []
        for u in range(group):
            a = a_ref[pl.ds(t0 + u, 1), :]
            b = b_ref[pl.ds(t0 + u, 1), :]
            g = g_ref[pl.ds(t0 + u, 1), :]
            g_hi = g.astype(BF16).astype(F32)
            g_lo = g - g_hi
            lefts.append(jnp.concatenate([jnp.where(sub == a, g_hi, 0.0), jnp.where(sub == a, g_lo, 0.0)], axis=1))
            onehot = jnp.where(sub == b, 1.0, 0.0)
            rights.append(jnp.concatenate([onehot, onehot], axis=1))
        o_ref[pl.ds(t0, group)] = _bdot(jnp.stack(lefts), jnp.stack(rights), _BNT)
        return carry

    lax.fori_loop(0, tt // group, body, 0)


def _peer_weights(a, b, g, tt=128):
    t, n = a.shape
    spec = pl.BlockSpec((tt, n), lambda i: (i, 0))
    return pl.pallas_call(
        _peer_weights_kernel,
        grid=(t // tt,),
        in_specs=[spec, spec, spec],
        out_specs=pl.BlockSpec((tt, PEER_NKEYS, PEER_NKEYS), lambda i: (i, 0, 0)),
        out_shape=jax.ShapeDtypeStruct((t, PEER_NKEYS, PEER_NKEYS), F32),
        compiler_params=_params("parallel"),
        name="peer_weights",
    )(a, b, g)


def _peer_act_kernel(h_ref, u_ref, w_ref, o_ref, *, ne1):
    tm = h_ref.shape[0]
    z = _dot(h_ref[...], u_ref[...], _NT)
    gelu = 0.5 * z * (1.0 + lax.erf(z * (2.0 ** -0.5)))
    w2 = w_ref.reshape(tm * ne1, PEER_NKEYS)
    w = jnp.concatenate([w2[pl.ds(j, tm, stride=ne1), :] for j in range(ne1)], axis=1)
    o_ref[...] = (gelu * w).astype(o_ref.dtype)


def _peer_act(h, u, layer, wts, tm=1024, te=1024):
    t, d = h.shape
    e = u.shape[1]
    ne1 = te // PEER_NKEYS
    return pl.pallas_call(
        functools.partial(_peer_act_kernel, ne1=ne1),
        grid=(t // tm, e // te),
        in_specs=[pl.BlockSpec((tm, d), lambda i, j: (i, 0), pipeline_mode=pl.Buffered(1)),
                  pl.BlockSpec((None, te, d), lambda i, j: (layer, j, 0)),
                  pl.BlockSpec((tm, ne1, PEER_NKEYS), lambda i, j: (i, j, 0))],
        out_specs=pl.BlockSpec((tm, te), lambda i, j: (i, j)),
        out_shape=jax.ShapeDtypeStruct((t, e), BF16),
        compiler_params=_params("parallel", "parallel"),
        name="peer_act",
    )(h, u, wts)


def kernel(x, c, w_in, w_branch, w_out, ada_w, ada_b, ada_table, hg_lb_logits, rw_mu, rw_w0, rw_w2, rw_a0, rw_a2, rw_g2, rw_k_k, rw_k_a, rw_r_k, rw_ln_w, rw_ln_b, rw_v0, rw_v1, rw_v2, fox_fb, peer_wq, peer_keys, peer_u, peer_v, final_norm_w):
    b, t, d = x.shape
    assert b == 1 and d == D_MODEL
    assert w_in.shape[2] == COL_GATE + N_BRANCH * d and COL_RW % 256 == 0 and COL_FOX % 256 == 0 and COL_FF % 256 == 0
    depth = w_in.shape[0]
    x = x.reshape(t, d)

    c8 = jnp.broadcast_to(c, (8, d))
    mods = _matmul(c8, ada_w, name="ada_mods", n_out=N_MOD * d, tm=8, tn=512, out_dtype=F32, pre="silu",
                   bias=ada_b.reshape(1, -1))[0].reshape(N_MOD, d)
    lbs = _hg_lower_bounds(hg_lb_logits)
    ret_tables = _retention_tables(t)
    w_in_t = jnp.swapaxes(w_in, 1, 2)
    w_ff_t = w_in_t[:, COL_FF:COL_GATE, :]
    u_bf, v_bf = peer_u.astype(BF16), peer_v.astype(BF16)
    zpad = lambda z, axis, n: jnp.pad(z, [(0, n - z.shape[i]) if i == axis else (0, 0) for i in range(z.ndim)])

    v_first = None
    for l in range(depth):
        m = mods + ada_table[l]
        sh1, sc1, g1, sh2, sc2, g2 = [m[i:i + 1] for i in range(N_MOD)]
        h = _rms_mod(x, 1.0 + sc1, sh1, BF16)
        p_rh = _matmul(h, w_in_t, name="proj_ret_hg", layer=l, wt_row0=0, n_out=COL_RW, tm=MM_TM, tn=MM_TN, out_dtype=F32)
        p_rw = _matmul(h, w_in_t, name="proj_rwkv", layer=l, wt_row0=COL_RW, n_out=RW_COLS, tm=MM_TM, tn=MM_TN // 2,
                       out_dtype=F32)
        p_fox = _matmul(h, w_in_t, name="proj_fox", layer=l, wt_row0=COL_FOX, n_out=COL_FF - COL_FOX, tm=MM_TM // 2,
                        tn=MM_TN, out_dtype=BF16)
        p_gate = _matmul(h, w_in_t, name="proj_gate", layer=l, wt_row0=COL_GATE, n_out=N_BRANCH * d, tm=MM_TM, tn=MM_TN,
                         out_dtype=F32)
        ya, yb = _retention_hgrn2(p_rh, ret_tables, lbs[l:l + 1])
        w2p = jnp.concatenate([rw_w2[l], jnp.zeros_like(rw_a2[l])], axis=0)
        a2p = jnp.concatenate([jnp.zeros_like(rw_w2[l]), rw_a2[l]], axis=0)
        vres = None
        if l > 0:
            vres = (v_first, rw_v0[l - 1][None], zpad(rw_v1[l - 1], 1, 128), zpad(rw_v2[l - 1], 0, 128))
        yc, v_own = _rwkv7(p_rw, rw_mu[l][None], rw_w0[l][None], w2p, rw_a0[l][None], a2p, rw_g2[l], vres, rw_k_k[l][None],
                           rw_k_a[l][None], rw_r_k[l].reshape(1, RW_W), rw_ln_w[l][None], rw_ln_b[l][None])
        if l == 0:
            v_first = v_own
        cum_row, cum_col = _fox_cum(h, w_ff_t[l], fox_fb[l])
        yd = _fox_attention(p_fox, cum_row, cum_col)
        merged = _merge([ya, yb, yc, yd], w_branch, l, p_gate)
        x = _matmul(merged, w_out, name="out_proj", layer=l, n_out=d, tm=MM_TM, tn=MM_TN, out_dtype=F32, res=x, gate=g1)
        h2 = _rms_mod(x, 1.0 + sc2, sh2, BF16)
        q = _matmul(h2, peer_wq, name="peer_query", layer=l, n_out=PEER_HEADS * PEER_DKEY, tm=MM_TM // 2, tn=MM_TN,
                    out_dtype=F32)
        ea, eb, eg = _peer_topk(q, peer_keys[l])
        wts = _peer_weights(ea, eb, eg)
        act = _peer_act(h2, u_bf, l, wts)
        x = _matmul_acc_res(act, v_bf, l, x, g2, tm=1024, tn=1024, tk=2048)
    out = _rms_mod(x, final_norm_w[None], jnp.zeros((1, d), F32), F32)
    return out.reshape(b, t, d)
```
